```python
import math
import jax
import jax.numpy as jnp
from jax import lax
import numpy as np

D_MODEL = 2048
BATCH = 4
SEQ = 2048
DEPTH = 4
DEC_BATCH = 16
DEC_SEQ = 2048
PAST_LEN = 128

GRID_W = 64
EPS = 1e-6
LB_FLOOR = 1e-30
SSM_WIDTH = D_MODEL // 4
SSM_GROUP = 16
SSM_GROUPS = SSM_WIDTH // SSM_GROUP
SSM_STATE = 64
HG_HEAD_DIM = 128
HG_WIDTH = D_MODEL // 4
HG_HEADS = HG_WIDTH // HG_HEAD_DIM
HG_CHUNK = 64
HEAD_DIM = 128
ATT_Q_HEADS = 8
ATT_KV_HEADS = 2
ATT_WIDTH = ATT_Q_HEADS * HEAD_DIM
KV_WIDTH = ATT_KV_HEADS * HEAD_DIM
Q_BLOCK = 128
ROPE_BASE = 10000.0
ROPE_AXIS_DIM = HEAD_DIM // 2
N_BRANCH = 3
MIX_WIDTH = SSM_WIDTH + HG_WIDTH + ATT_WIDTH
D_FF = 5632
SPLIT_SIZES = (SSM_WIDTH, HG_WIDTH, HG_WIDTH, HG_WIDTH, HG_WIDTH, HG_WIDTH, ATT_WIDTH, KV_WIDTH, KV_WIDTH, N_BRANCH * D_MODEL)
IN_WIDTH = SSM_WIDTH + 5 * HG_WIDTH + ATT_WIDTH + 2 * KV_WIDTH + N_BRANCH * D_MODEL

kernel_name = "hybrid_s5_hgrn2_gqa_encoder"


def rms_norm(x, gain):
    xf = x.astype(jnp.float32)
    y = xf * lax.rsqrt(jnp.mean(xf * xf, axis=-1, keepdims=True) + EPS)
    return (y * gain.astype(jnp.float32)).astype(x.dtype)


def swiglu_ffn(x, w_gate_up, w_down):
    gate, up = jnp.split(x @ w_gate_up, 2, axis=-1)
    return (jax.nn.silu(gate) * up) @ w_down


def axial_rope_tables(seq_len):
    rows = seq_len // GRID_W
    row_idx = jnp.broadcast_to(jnp.arange(rows, dtype=jnp.float32)[:, None], (rows, GRID_W)).reshape(-1)
    col_idx = jnp.broadcast_to(jnp.arange(GRID_W, dtype=jnp.float32)[None, :], (rows, GRID_W)).reshape(-1)
    inv_freq = ROPE_BASE ** (-jnp.arange(0, ROPE_AXIS_DIM, 2, dtype=jnp.float32) / ROPE_AXIS_DIM)
    ang_row = row_idx[:, None] * inv_freq[None, :]
    ang_col = col_idx[:, None] * inv_freq[None, :]
    return (jnp.cos(ang_row), jnp.sin(ang_row), jnp.cos(ang_col), jnp.sin(ang_col))


def _rotate(x, cos, sin):
    x1, x2 = jnp.split(x, 2, axis=-1)
    c = cos[None, :, None, :]
    s = sin[None, :, None, :]
    return jnp.concatenate([x1 * c - x2 * s, x2 * c + x1 * s], axis=-1)


def apply_axial_rope(x, tables):
    cos_r, sin_r, cos_c, sin_c = tables
    xf = x.astype(jnp.float32)
    out = jnp.concatenate([_rotate(xf[..., :ROPE_AXIS_DIM], cos_r, sin_r),
                           _rotate(xf[..., ROPE_AXIS_DIM:], cos_c, sin_c)], axis=-1)
    return out.astype(x.dtype)


def gqa_block_attention(q, k, v):
    bsz, seq_len = q.shape[:2]
    n_blocks = seq_len // Q_BLOCK
    group = ATT_Q_HEADS // ATT_KV_HEADS
    q_blocks = q.reshape(bsz, n_blocks, Q_BLOCK, ATT_KV_HEADS, group, HEAD_DIM).transpose(1, 0, 2, 3, 4, 5)
    scale = HEAD_DIM ** -0.5

    def attend(q_blk):
        s = jnp.einsum('bqhgd,bkhd->bhgqk', q_blk, k, preferred_element_type=jnp.float32) * scale
        p = jax.nn.softmax(s, axis=-1).astype(v.dtype)
        return jnp.einsum('bhgqk,bkhd->bqhgd', p, v)

    out = lax.map(attend, q_blocks)
    return out.transpose(1, 0, 2, 3, 4, 5).reshape(bsz, seq_len, ATT_WIDTH)


def s5_discretize(lam_re, lam_im, log_dt, b_re, b_im):
    dt = jnp.exp(log_dt)[:, None]
    mag = jnp.exp(lam_re * dt)
    a_re = mag * jnp.cos(lam_im * dt)
    a_im = mag * jnp.sin(lam_im * dt)
    den = lam_re * lam_re + lam_im * lam_im
    num_re = a_re - 1.0
    coef_re = (num_re * lam_re + a_im * lam_im) / den
    coef_im = (a_im * lam_re - num_re * lam_im) / den
    bb_re = coef_re[..., None] * b_re - coef_im[..., None] * b_im
    bb_im = coef_re[..., None] * b_im + coef_im[..., None] * b_re
    return a_re, a_im, bb_re, bb_im


def _complex_affine_combine(earlier, later):
    a1r, a1i, b1r, b1i = earlier
    a2r, a2i, b2r, b2i = later
    return (a2r * a1r - a2i * a1i, a2r * a1i + a2i * a1r,
            a2r * b1r - a2i * b1i + b2r, a2r * b1i + a2i * b1r + b2i)


def complex_diag_scan(a_re, a_im, bu_re, bu_im):
    seq_len = bu_re.shape[1]
    a_re_t = jnp.broadcast_to(a_re, (seq_len,) + a_re.shape)
    a_im_t = jnp.broadcast_to(a_im, (seq_len,) + a_im.shape)

    def one_sequence(br, bi):
        _, _, xr, xi = lax.associative_scan(_complex_affine_combine, (a_re_t, a_im_t, br, bi), axis=0)
        return xr, xi

    return jax.vmap(one_sequence)(bu_re, bu_im)


def s5_bidirectional(u, lam_re, lam_im, log_dt, b_re, b_im, c_re, c_im, d_skip, w_glu):
    bsz, seq_len, _ = u.shape
    uf = u.astype(jnp.float32)
    ug = uf.reshape(bsz, seq_len, SSM_GROUPS, SSM_GROUP)
    y = d_skip.astype(jnp.float32) * uf
    for direction in range(2):
        a_re, a_im, bb_re, bb_im = s5_discretize(
            lam_re[direction].astype(jnp.float32), lam_im[direction].astype(jnp.float32),
            log_dt[direction].astype(jnp.float32), b_re[direction].astype(jnp.float32),
            b_im[direction].astype(jnp.float32))
        ud = ug if direction == 0 else jnp.flip(ug, axis=1)
        bu_re = jnp.einsum('blgp,gnp->blgn', ud, bb_re)
        bu_im = jnp.einsum('blgp,gnp->blgn', ud, bb_im)
        xr, xi = complex_diag_scan(a_re, a_im, bu_re, bu_im)
        yd = (jnp.einsum('blgn,gpn->blgp', xr, c_re[direction].astype(jnp.float32))
              - jnp.einsum('blgn,gpn->blgp', xi, c_im[direction].astype(jnp.float32)))
        if direction == 1:
            yd = jnp.flip(yd, axis=1)
        y = y + yd.reshape(bsz, seq_len, SSM_WIDTH)
    z = jax.nn.gelu(y).astype(u.dtype)
    return z * jax.nn.sigmoid(z @ w_glu)


def _hgrn2_chunk_scan(q, k, v, log_f):
    bsz, seq_len, n_heads, dk = q.shape
    dv = v.shape[-1]
    n_chunks = seq_len // HG_CHUNK

    def to_chunks(t):
        return t.reshape(bsz, n_chunks, HG_CHUNK, n_heads, t.shape[-1]).transpose(1, 0, 3, 2, 4)

    causal = jnp.tril(jnp.ones((HG_CHUNK, HG_CHUNK), dtype=bool))[:, :, None]

    def step(state, inp):
        qc, kc, vc, lf = inp
        cum = jnp.cumsum(lf, axis=2)
        diff = cum[:, :, :, None, :] - cum[:, :, None, :, :]
        decay = jnp.where(causal, jnp.exp(jnp.where(causal, diff, 0.0)), 0.0)
        scores = jnp.einsum('bhjd,bhsd,bhjsd->bhjs', qc, kc, decay)
        out = (jnp.einsum('bhjs,bhsv->bhjv', scores, vc)
               + jnp.einsum('bhjd,bhdv->bhjv', qc * jnp.exp(cum), state))
        last = cum[:, :, -1:, :]
        new_state = (jnp.exp(last)[:, :, 0, :, None] * state
                     + jnp.einsum('bhsd,bhsv->bhdv', kc * jnp.exp(last - cum), vc))
        return new_state, out

    state0 = jnp.zeros((bsz, n_heads, dk, dv), jnp.float32)
    _, out = lax.scan(step, state0, (to_chunks(q), to_chunks(k), to_chunks(v), to_chunks(log_f)))
    return out.transpose(1, 0, 3, 2, 4).reshape(bsz, seq_len, n_heads, dv)


def hgrn2_bidirectional(q_raw, f_raw_fwd, f_raw_bwd, i_raw, g_raw, lower_bound, out_gain):
    bsz, seq_len, _ = q_raw.shape

    def heads(t):
        return t.astype(jnp.float32).reshape(bsz, seq_len, HG_HEADS, HG_HEAD_DIM)

    def flip(t):
        return jnp.flip(t, axis=1)

    q = jax.nn.silu(heads(q_raw))
    v = heads(i_raw)
    out = jnp.zeros((bsz, seq_len, HG_HEADS, HG_HEAD_DIM), jnp.float32)
    for direction, f_raw in enumerate((f_raw_fwd, f_raw_bwd)):
        lb = lower_bound[direction].astype(jnp.float32).reshape(HG_HEADS, HG_HEAD_DIM)
        z = heads(f_raw)
        log_f = jnp.logaddexp(jnp.log(jnp.maximum(lb, LB_FLOOR)), jnp.log1p(-lb) + jax.nn.log_sigmoid(z))
        k = (1.0 - lb) * jax.nn.sigmoid(-z)
        if direction == 0:
            out = out + _hgrn2_chunk_scan(q, k, v, log_f)
        else:
            out = out + flip(_hgrn2_chunk_scan(flip(q), flip(k), flip(v), flip(log_f)))
    gated = rms_norm(out, out_gain) * jax.nn.silu(heads(g_raw))
    return gated.reshape(bsz, seq_len, HG_WIDTH).astype(q_raw.dtype)


def encoder_layer(x, rope, ffn1_norm, ffn1_w_gate_up, ffn1_w_down, mix_norm, w_in,
                  lam_re, lam_im, log_dt, b_re, b_im, c_re, c_im, d_skip, w_glu,
                  lower_bound, hg_out_norm, q_norm, k_norm, w_branch, w_out,
                  ffn2_norm, ffn2_w_gate_up, ffn2_w_down):
    bsz, seq_len, _ = x.shape
    h = x + 0.5 * swiglu_ffn(rms_norm(x, ffn1_norm), ffn1_w_gate_up, ffn1_w_down)
    u = rms_norm(h, mix_norm)
    offsets = np.cumsum(SPLIT_SIZES)[:-1].tolist()
    (u_ssm, hg_q, hg_f_fwd, hg_f_bwd, hg_i, hg_g, att_q, att_k, att_v, gate_logits) = jnp.split(u @ w_in, offsets, axis=-1)
    y_ssm = s5_bidirectional(u_ssm, lam_re, lam_im, log_dt, b_re, b_im, c_re, c_im, d_skip, w_glu)
    y_hg = hgrn2_bidirectional(hg_q, hg_f_fwd, hg_f_bwd, hg_i, hg_g, lower_bound, hg_out_norm)
    q = apply_axial_rope(rms_norm(att_q.reshape(bsz, seq_len, ATT_Q_HEADS, HEAD_DIM), q_norm), rope)
    k = apply_axial_rope(rms_norm(att_k.reshape(bsz, seq_len, ATT_KV_HEADS, HEAD_DIM), k_norm), rope)
    v = att_v.reshape(bsz, seq_len, ATT_KV_HEADS, HEAD_DIM)
    y_att = gqa_block_attention(q, k, v)
    gates = jax.nn.sigmoid(gate_logits.reshape(bsz, seq_len, N_BRANCH, D_MODEL))
    w_ssm, w_hg, w_att = jnp.split(w_branch, [SSM_WIDTH, SSM_WIDTH + HG_WIDTH], axis=0)
    merged = (gates[:, :, 0] * (y_ssm @ w_ssm)
              + gates[:, :, 1] * (y_hg @ w_hg)
              + gates[:, :, 2] * (y_att @ w_att))
    h = h + merged @ w_out
    return h + 0.5 * swiglu_ffn(rms_norm(h, ffn2_norm), ffn2_w_gate_up, ffn2_w_down)


def encoder_trunk(x, stacked):
    rope = axial_rope_tables(x.shape[1])
    for layer in range(DEPTH):
        x = encoder_layer(x, rope, *[p[layer] for p in stacked])
    return x


def setup_inputs(seed: int = 0) -> dict:
    key = jax.random.key(seed)
    ks = jax.random.split(key, 32)
    f32 = jnp.float32

    def normal(k, shape, scale):
        return scale * jax.random.normal(k, shape, f32)

    def gain(k, shape):
        return 1.0 + 0.01 * jax.random.normal(k, shape, f32)

    ssm_shape = (DEPTH, 2, SSM_GROUPS, SSM_STATE)
    n_idx = jnp.arange(SSM_STATE, dtype=f32)
    return {
        "x_prompt": normal(ks[0], (BATCH, SEQ, D_MODEL), 1.0),
        "x_sample": normal(ks[1], (DEC_BATCH, DEC_SEQ, D_MODEL), 1.0),
        "ffn1_norm": gain(ks[2], (DEPTH, D_MODEL)),
        "ffn1_w_gate_up": normal(ks[3], (DEPTH, D_MODEL, 2 * D_FF), D_MODEL ** -0.5),
        "ffn1_w_down": normal(ks[4], (DEPTH, D_FF, D_MODEL), D_FF ** -0.5),
        "mix_norm": gain(ks[5], (DEPTH, D_MODEL)),
        "w_in": normal(ks[6], (DEPTH, D_MODEL, IN_WIDTH), D_MODEL ** -0.5),
        "ssm_lambda_re": -0.5 + normal(ks[7], ssm_shape, 0.01),
        "ssm_lambda_im": math.pi * n_idx + normal(ks[8], ssm_shape, 0.01),
        "ssm_log_dt": jax.random.uniform(ks[9], (DEPTH, 2, SSM_GROUPS), f32, math.log(1e-3), math.log(1e-1)),
        "ssm_b_re": normal(ks[10], (DEPTH, 2, SSM_GROUPS, SSM_STATE, SSM_GROUP), SSM_GROUP ** -0.5),
        "ssm_b_im": normal(ks[11], (DEPTH, 2, SSM_GROUPS, SSM_STATE, SSM_GROUP), SSM_GROUP ** -0.5),
        "ssm_c_re": normal(ks[12], (DEPTH, 2, SSM_GROUPS, SSM_GROUP, SSM_STATE), SSM_STATE ** -0.5),
        "ssm_c_im": normal(ks[13], (DEPTH, 2, SSM_GROUPS, SSM_GROUP, SSM_STATE), SSM_STATE ** -0.5),
        "ssm_d": normal(ks[14], (DEPTH, SSM_WIDTH), 1.0),
        "ssm_w_glu": normal(ks[15], (DEPTH, SSM_WIDTH, SSM_WIDTH), SSM_WIDTH ** -0.5),
        "hg_lb_logits": normal(ks[16], (DEPTH, 2, HG_WIDTH), 1.0),
        "hg_out_norm": gain(ks[17], (DEPTH, HG_HEAD_DIM)),
        "att_q_norm": gain(ks[18], (DEPTH, HEAD_DIM)),
        "att_k_norm": gain(ks[19], (DEPTH, HEAD_DIM)),
        "w_branch": normal(ks[20], (DEPTH, MIX_WIDTH, D_MODEL), MIX_WIDTH ** -0.5),
        "w_out": normal(ks[21], (DEPTH, D_MODEL, D_MODEL), D_MODEL ** -0.5),
        "ffn2_norm": gain(ks[22], (DEPTH, D_MODEL)),
        "ffn2_w_gate_up": normal(ks[23], (DEPTH, D_MODEL, 2 * D_FF), D_MODEL ** -0.5),
        "ffn2_w_down": normal(ks[24], (DEPTH, D_FF, D_MODEL), D_FF ** -0.5),
    }


def reference(x_prompt, x_sample, ffn1_norm, ffn1_w_gate_up, ffn1_w_down, mix_norm, w_in,
              ssm_lambda_re, ssm_lambda_im, ssm_log_dt, ssm_b_re, ssm_b_im, ssm_c_re, ssm_c_im,
              ssm_d, ssm_w_glu, hg_lb_logits, hg_out_norm, att_q_norm, att_k_norm,
              w_branch, w_out, ffn2_norm, ffn2_w_gate_up, ffn2_w_down):
    lb_p = jax.nn.softmax(hg_lb_logits.astype(jnp.float32), axis=0)
    hg_lower_bound = jnp.clip(jnp.cumsum(lb_p, axis=0) - lb_p[:1], 0.0, 1.0 - 1e-6)
    stacked = (ffn1_norm, ffn1_w_gate_up, ffn1_w_down, mix_norm, w_in,
               ssm_lambda_re, ssm_lambda_im, ssm_log_dt, ssm_b_re, ssm_b_im, ssm_c_re, ssm_c_im,
               ssm_d, ssm_w_glu, hg_lower_bound, hg_out_norm, att_q_norm, att_k_norm,
               w_branch, w_out, ffn2_norm, ffn2_w_gate_up, ffn2_w_down)
    y_prompt = encoder_trunk(x_prompt, stacked)
    y_sample = encoder_trunk(x_sample, stacked)
    return (y_prompt, y_sample)
```

```python
import functools
import math

import jax
import jax.numpy as jnp
from jax import lax
from jax.experimental import pallas as pl
from jax.experimental.pallas import tpu as pltpu

F32 = jnp.float32
BF16 = jnp.bfloat16

D_MODEL = 2048
DEPTH = 4
GRID_W = 64
EPS = 1e-6
LB_FLOOR = 1e-30
SSM_WIDTH = 512
SSM_GROUP = 16
SSM_GROUPS = 32
SSM_STATE = 64
SSM_STATES = SSM_GROUPS * SSM_STATE
HG_HEAD_DIM = 128
HG_WIDTH = 512
HG_HEADS = 4
HEAD_DIM = 128
ATT_Q_HEADS = 8
ATT_KV_HEADS = 2
ATT_GROUP = ATT_Q_HEADS // ATT_KV_HEADS
ATT_WIDTH = 1024
KV_WIDTH = 256
ROPE_BASE = 10000.0
ROPE_AXIS_DIM = HEAD_DIM // 2
N_BRANCH = 3
MIX_WIDTH = 2048
D_FF = 5632
IN_WIDTH = SSM_WIDTH + 5 * HG_WIDTH + ATT_WIDTH + 2 * KV_WIDTH + N_BRANCH * D_MODEL

OFF_SSM = 0
OFF_HG_Q = 512
OFF_HG_FF = 1024
OFF_HG_FB = 1536
OFF_HG_I = 2048
OFF_HG_G = 2560
OFF_ATT_Q = 3072
OFF_ATT_K = 4096
OFF_ATT_V = 4352
OFF_GATE = 4608

LANES = 128
SLABS = SSM_STATES // LANES

TOKEN_TILE = 512
FF_TILE = 512
PROJ_TILE = 512
MERGE_TILE = 512
ATT_Q_TILE = 256
S5_CHUNK = 256
S5_PITCH = S5_CHUNK + 8
HG_CHUNK = 128
HG_SUB = 16

MIB = 1024 * 1024


def _params(semantics, vmem_mib):
    return pltpu.CompilerParams(dimension_semantics=semantics, vmem_limit_bytes=vmem_mib * MIB)


def _resident(shape, index_map):
    return pl.BlockSpec(shape, index_map, pipeline_mode=pl.Buffered(1))


def _rms_scale(x):
    return x * lax.rsqrt(jnp.mean(x * x, axis=-1, keepdims=True) + EPS)


def _ffn_kernel(x_ref, g_ref, wg_ref, wu_ref, wd_ref, o_ref, xn_ref, acc_ref):
    f = pl.program_id(1)

    @pl.when(f == 0)
    def _():
        xn_ref[...] = (_rms_scale(x_ref[...]) * g_ref[...]).astype(BF16)
        acc_ref[...] = jnp.zeros_like(acc_ref)

    xn = xn_ref[...]
    gate = jnp.dot(xn, wg_ref[...], preferred_element_type=F32)
    up = jnp.dot(xn, wu_ref[...], preferred_element_type=F32)
    act = (gate * jax.nn.sigmoid(gate) * up).astype(BF16)
    acc_ref[...] += jnp.dot(act, wd_ref[...], preferred_element_type=F32)

    @pl.when(f == pl.num_programs(1) - 1)
    def _():
        o_ref[...] = x_ref[...] + 0.5 * acc_ref[...]


def _ffn(x, gain, w_gate_up, w_down):
    t, d = x.shape
    d_ff = w_down.shape[0]
    tm = min(TOKEN_TILE, t)
    nf = d_ff // FF_TILE
    return pl.pallas_call(
        _ffn_kernel,
        grid=(t // tm, nf),
        in_specs=[
            pl.BlockSpec((tm, d), lambda i, f: (i, 0)),
            pl.BlockSpec((1, d), lambda i, f: (0, 0)),
            pl.BlockSpec((d, FF_TILE), lambda i, f: (0, f)),
            pl.BlockSpec((d, FF_TILE), lambda i, f: (0, f + nf)),
            pl.BlockSpec((FF_TILE, d), lambda i, f: (f, 0)),
        ],
        out_specs=pl.BlockSpec((tm, d), lambda i, f: (i, 0)),
        out_shape=jax.ShapeDtypeStruct((t, d), F32),
        scratch_shapes=[pltpu.VMEM((tm, d), BF16), pltpu.VMEM((tm, d), F32)],
        compiler_params=_params(("parallel", "arbitrary"), 48),
        name="ffn",
    )(x, gain.reshape(1, d), w_gate_up, w_gate_up, w_down)


def _in_proj_kernel(h_ref, g_ref, w_ref, o_ref, xn_ref):
    @pl.when(pl.program_id(1) == 0)
    def _():
        xn_ref[...] = (_rms_scale(h_ref[...]) * g_ref[...]).astype(BF16)

    o_ref[...] = jnp.dot(xn_ref[...], w_ref[...], preferred_element_type=F32)


def _in_proj(h, gain, w_in):
    t, d = h.shape
    n = w_in.shape[1]
    tm = min(TOKEN_TILE, t)
    return pl.pallas_call(
        _in_proj_kernel,
        grid=(t // tm, n // PROJ_TILE),
        in_specs=[
            pl.BlockSpec((tm, d), lambda i, j: (i, 0)),
            pl.BlockSpec((1, d), lambda i, j: (0, 0)),
            pl.BlockSpec((d, PROJ_TILE), lambda i, j: (0, j)),
        ],
        out_specs=pl.BlockSpec((tm, PROJ_TILE), lambda i, j: (i, j)),
        out_shape=jax.ShapeDtypeStruct((t, n), F32),
        scratch_shapes=[pltpu.VMEM((tm, d), BF16)],
        compiler_params=_params(("parallel", "arbitrary"), 32),
        name="in_proj",
    )(h, gain.reshape(1, d), w_in)


def _lower_bound_kernel(logit_ref, lb_ref):
    x = logit_ref[...]
    e = jnp.exp(x - jnp.max(x, axis=0, keepdims=True))
    p = e / jnp.sum(e, axis=0, keepdims=True)
    run = jnp.zeros_like(p[0:1])
    for layer in range(DEPTH):
        run = run + p[layer:layer + 1]
        lb_ref[layer:layer + 1, :] = jnp.clip(run - p[0:1], 0.0, 1.0 - 1e-6)


def _hg_lower_bound(logits):
    flat = logits.astype(F32).reshape(DEPTH, 2 * HG_WIDTH)
    lb = pl.pallas_call(
        _lower_bound_kernel,
        out_shape=jax.ShapeDtypeStruct(flat.shape, F32),
        name="hg_lower_bound",
    )(flat)
    return lb.reshape(DEPTH, 2, HG_WIDTH)


def _s5_prep_kernel(lre_ref, lim_ref, ldt_ref, bre_ref, bim_ref, a_ref, bbar_ref):
    lre = lre_ref[0]
    lim = lim_ref[0]
    dt = jnp.exp(ldt_ref[0])
    mag = jnp.exp(lre * dt)
    a_re = mag * jnp.cos(lim * dt)
    a_im = mag * jnp.sin(lim * dt)
    den = lre * lre + lim * lim
    num_re = a_re - 1.0
    coef_re = (num_re * lre + a_im * lim) / den
    coef_im = (a_im * lre - num_re * lim) / den
    a_ref[0, 0:1, :] = a_re
    a_ref[0, 1:2, :] = a_im
    bre = bre_ref[0]
    bim = bim_ref[0]
    bbar_ref[0, :, 0:SSM_STATES] = (coef_re * bre - coef_im * bim).astype(BF16)
    bbar_ref[0, :, SSM_STATES:] = (coef_re * bim + coef_im * bre).astype(BF16)


def _block_diag_in(b):
    eye = jnp.eye(SSM_GROUPS, dtype=b.dtype)
    full = jnp.einsum('xgnp,gh->xgphn', b, eye)
    return full.reshape(b.shape[0], SSM_WIDTH, SSM_STATES)


def _block_diag_out(c):
    eye = jnp.eye(SSM_GROUPS, dtype=c.dtype)
    full = jnp.einsum('xgpn,gh->xgnhp', c, eye)
    return full.reshape(c.shape[0], SSM_STATES, SSM_WIDTH)


def _s5_prepare(lam_re, lam_im, log_dt, b_re, b_im, c_re, c_im):
    n = DEPTH * 2
    flat = lambda a: a.astype(F32).reshape(n, 1, SSM_STATES)
    ldt = jnp.repeat(log_dt.astype(F32).reshape(n, SSM_GROUPS), SSM_STATE, axis=-1)
    bre = _block_diag_in(b_re.astype(F32).reshape(n, SSM_GROUPS, SSM_STATE, SSM_GROUP))
    bim = _block_diag_in(b_im.astype(F32).reshape(n, SSM_GROUPS, SSM_STATE, SSM_GROUP))
    vec = pl.BlockSpec((1, 1, SSM_STATES), lambda i: (i, 0, 0))
    mat = pl.BlockSpec((1, SSM_WIDTH, SSM_STATES), lambda i: (i, 0, 0))
    a, bbar = pl.pallas_call(
        _s5_prep_kernel,
        grid=(n,),
        in_specs=[vec, vec, vec, mat, mat],
        out_specs=[pl.BlockSpec((1, 2, SSM_STATES), lambda i: (i, 0, 0)),
                   pl.BlockSpec((1, SSM_WIDTH, 2 * SSM_STATES), lambda i: (i, 0, 0))],
        out_shape=[jax.ShapeDtypeStruct((n, 2, SSM_STATES), F32),
                   jax.ShapeDtypeStruct((n, SSM_WIDTH, 2 * SSM_STATES), BF16)],
        compiler_params=_params(("parallel",), 48),
        name="s5_prep",
    )(flat(lam_re), flat(lam_im), ldt.reshape(n, 1, SSM_STATES), bre, bim)
    a = a.reshape(DEPTH, 2, 2, SLABS, LANES)
    bbar = bbar.reshape(DEPTH, 2, SSM_WIDTH, 2 * SSM_STATES)
    cre = _block_diag_out(c_re.astype(F32).reshape(n, SSM_GROUPS, SSM_GROUP, SSM_STATE))
    cim = _block_diag_out(c_im.astype(F32).reshape(n, SSM_GROUPS, SSM_GROUP, SSM_STATE))
    cmat = jnp.stack([cre, cim], axis=1).astype(BF16)
    return a, bbar, cmat.reshape(DEPTH, 2, 2, SSM_STATES, SSM_WIDTH)


def _s5_kernel(u_ref, a_ref, bbar_ref, c_ref, d_ref, wglu_ref, o_ref, y_ref, sf_ref, sb_ref):
    seq_len = u_ref.shape[0]
    q = min(S5_CHUNK, seq_len)
    n_chunks = seq_len // q
    im0 = SLABS * S5_PITCH
    y_ref[...] = d_ref[...] * u_ref[...]
    a_f_re, a_f_im = a_ref[0, 0], a_ref[0, 1]
    a_b_re, a_b_im = a_ref[1, 0], a_ref[1, 1]

    def project_in(rows, direction, slab_ref):
        bu = jnp.dot(u_ref[rows, :].astype(BF16), bbar_ref[direction], preferred_element_type=F32)
        for j in range(2 * SLABS):
            slab_ref[j * S5_PITCH:j * S5_PITCH + q, :] = bu[:, j * LANES:(j + 1) * LANES]

    def project_out(direction, slab_ref):
        def natural(first):
            return jnp.concatenate(
                [slab_ref[(first + j) * S5_PITCH:(first + j) * S5_PITCH + q, :] for j in range(SLABS)],
                axis=1).astype(BF16)
        return (jnp.dot(natural(0), c_ref[direction, 0], preferred_element_type=F32)
                - jnp.dot(natural(SLABS), c_ref[direction, 1], preferred_element_type=F32))

    def advance(slab_ref, t, a_re, a_im, x_re, x_im):
        re_rows = pl.ds(t, SLABS, stride=S5_PITCH)
        im_rows = pl.ds(im0 + t, SLABS, stride=S5_PITCH)
        n_re = a_re * x_re - a_im * x_im + slab_ref[re_rows, :]
        n_im = a_re * x_im + a_im * x_re + slab_ref[im_rows, :]
        slab_ref[re_rows, :] = n_re
        slab_ref[im_rows, :] = n_im
        return n_re, n_im

    def chunk(c, carry):
        rows_f = pl.ds(pl.multiple_of(c * q, q), q)
        rows_b = pl.ds(pl.multiple_of((n_chunks - 1 - c) * q, q), q)
        project_in(rows_f, 0, sf_ref)
        project_in(rows_b, 1, sb_ref)

        def step(i, st):
            f_re, f_im = advance(sf_ref, i, a_f_re, a_f_im, st[0], st[1])
            b_re, b_im = advance(sb_ref, q - 1 - i, a_b_re, a_b_im, st[2], st[3])
            return f_re, f_im, b_re, b_im

        carry = lax.fori_loop(0, q, step, carry, unroll=4)
        y_ref[rows_f, :] += project_out(0, sf_ref)
        y_ref[rows_b, :] += project_out(1, sb_ref)
        return carry

    zero = jnp.zeros((SLABS, LANES), F32)
    lax.fori_loop(0, n_chunks, chunk, (zero, zero, zero, zero))
    z = jax.nn.gelu(y_ref[...])
    glu = jnp.dot(z.astype(BF16), wglu_ref[...], preferred_element_type=F32)
    o_ref[...] = (z * jax.nn.sigmoid(glu)).astype(o_ref.dtype)


def _s5(proj, seq_len, a, bbar, cmat, d_skip, w_glu):
    t = proj.shape[0]
    whole = lambda shape: _resident(shape, lambda s: (0,) * len(shape))
    return pl.pallas_call(
        _s5_kernel,
        grid=(t // seq_len,),
        in_specs=[
            pl.BlockSpec((seq_len, SSM_WIDTH), lambda s: (s, OFF_SSM // SSM_WIDTH)),
            whole((2, 2, SLABS, LANES)),
            whole((2, SSM_WIDTH, 2 * SSM_STATES)),
            whole((2, 2, SSM_STATES, SSM_WIDTH)),
            whole((1, SSM_WIDTH)),
            whole((SSM_WIDTH, SSM_WIDTH)),
        ],
        out_specs=pl.BlockSpec((seq_len, SSM_WIDTH), lambda s: (s, 0)),
        out_shape=jax.ShapeDtypeStruct((t, SSM_WIDTH), BF16),
        scratch_shapes=[pltpu.VMEM((seq_len, SSM_WIDTH), F32),
                        pltpu.VMEM((2 * SLABS * S5_PITCH, LANES), F32),
                        pltpu.VMEM((2 * SLABS * S5_PITCH, LANES), F32)],
        compiler_params=_params(("parallel",), 56),
        name="s5",
    )(proj, a, bbar, cmat, d_skip.reshape(1, SSM_WIDTH), w_glu)


def _hg_chunk(qc, kc, vc, lfc, state_t, reverse):
    c = qc.shape[0]
    row = lax.broadcasted_iota(jnp.int32, (c, c), 0)
    col = lax.broadcasted_iota(jnp.int32, (c, c), 1)
    earlier = (col >= row) if reverse else (col <= row)
    cum = jnp.dot(earlier.astype(F32), lfc, precision=lax.Precision.HIGHEST,
                  preferred_element_type=F32)
    cum_before = cum - lfc
    total = cum[0:1] if reverse else cum[c - 1:c]

    q_state = (qc * jnp.exp(cum)).astype(BF16)
    out = lax.dot_general(q_state, state_t.astype(BF16), (((1,), (1,)), ((), ())),
                          preferred_element_type=F32)
    k_state = (kc * jnp.exp(total - cum)).astype(BF16)
    new_state = state_t * jnp.exp(total) + jnp.dot(vc.T.astype(BF16), k_state,
                                                   preferred_element_type=F32)

    key_row = lax.broadcasted_iota(jnp.int32, (c, 1), 0)
    score_rows = []
    for blk in range(c // HG_SUB):
        lo, hi = blk * HG_SUB, (blk + 1) * HG_SUB
        ref_row = cum_before[hi - 1:hi] if reverse else cum_before[lo:lo + 1]
        q_blk = (qc[lo:hi] * jnp.exp(cum[lo:hi] - ref_row)).astype(BF16)
        reach = (key_row >= lo) if reverse else (key_row < hi)
        k_blk = (kc * jnp.exp(jnp.where(reach, ref_row - cum, 0.0))).astype(BF16)
        score_rows.append(lax.dot_general(q_blk, k_blk, (((1,), (1,)), ((), ())),
                                          preferred_element_type=F32))
    scores = jnp.where(earlier, jnp.concatenate(score_rows, axis=0), 0.0)
    out = out + jnp.dot(scores.astype(BF16), vc.astype(BF16), preferred_element_type=F32)
    return out, new_state


def _hgrn_kernel(q_ref, ff_ref, fb_ref, i_ref, g_ref, lb_ref, gain_ref, o_ref,
                 qs_ref, kf_ref, lff_ref, kb_ref, lfb_ref, acc_ref):
    seq_len = q_ref.shape[0]
    c = min(HG_CHUNK, seq_len)
    n_chunks = seq_len // c
    q_raw = q_ref[...]
    qs_ref[...] = q_raw * jax.nn.sigmoid(q_raw)
    for direction, (f_ref, k_ref, lf_ref) in enumerate(((ff_ref, kf_ref, lff_ref),
                                                        (fb_ref, kb_ref, lfb_ref))):
        lb = lb_ref[direction:direction + 1, :]
        z = f_ref[...]
        log_sig = jnp.minimum(z, 0.0) - jnp.log1p(jnp.exp(-jnp.abs(z)))
        a = jnp.log(jnp.maximum(lb, LB_FLOOR))
        b = jnp.log1p(-lb) + log_sig
        lf_ref[...] = jnp.maximum(a, b) + jnp.log1p(jnp.exp(-jnp.abs(a - b)))
        k_ref[...] = (1.0 - lb) * jax.nn.sigmoid(-z)
    acc_ref[...] = jnp.zeros_like(acc_ref)

    def chunk(ci, states):
        rows_f = pl.ds(pl.multiple_of(ci * c, c), c)
        rows_b = pl.ds(pl.multiple_of((n_chunks - 1 - ci) * c, c), c)
        out_f, st_f = _hg_chunk(qs_ref[rows_f, :], kf_ref[rows_f, :], i_ref[rows_f, :],
                                lff_ref[rows_f, :], states[0], False)
        out_b, st_b = _hg_chunk(qs_ref[rows_b, :], kb_ref[rows_b, :], i_ref[rows_b, :],
                                lfb_ref[rows_b, :], states[1], True)
        acc_ref[rows_f, :] += out_f
        acc_ref[rows_b, :] += out_b
        return st_f, st_b

    zero = jnp.zeros((HG_HEAD_DIM, HG_HEAD_DIM), F32)
    lax.fori_loop(0, n_chunks, chunk, (zero, zero))
    g_raw = g_ref[...]
    normed = _rms_scale(acc_ref[...]) * gain_ref[...]
    o_ref[...] = (normed * (g_raw * jax.nn.sigmoid(g_raw))).astype(o_ref.dtype)


def _hgrn(proj, seq_len, lower_bound, out_gain):
    t = proj.shape[0]
    col = lambda off: pl.BlockSpec((seq_len, HG_HEAD_DIM),
                                   lambda s, h, off=off: (s, off // HG_HEAD_DIM + h))
    head = pltpu.VMEM((seq_len, HG_HEAD_DIM), F32)
    return pl.pallas_call(
        _hgrn_kernel,
        grid=(t // seq_len, HG_HEADS),
        in_specs=[col(OFF_HG_Q), col(OFF_HG_FF), col(OFF_HG_FB), col(OFF_HG_I), col(OFF_HG_G),
                  pl.BlockSpec((2, HG_HEAD_DIM), lambda s, h: (0, h)),
                  pl.BlockSpec((1, HG_HEAD_DIM), lambda s, h: (0, 0))],
        out_specs=pl.BlockSpec((seq_len, HG_HEAD_DIM), lambda s, h: (s, h)),
        out_shape=jax.ShapeDtypeStruct((t, HG_WIDTH), BF16),
        scratch_shapes=[head] * 6,
        compiler_params=_params(("parallel", "parallel"), 32),
        name="hgrn2",
    )(proj, proj, proj, proj, proj, lower_bound, out_gain.reshape(1, HG_HEAD_DIM))


def _rope_tables(seq_len):
    pos = jnp.arange(seq_len, dtype=jnp.int32)
    row = (pos // GRID_W).astype(F32)
    colp = (pos % GRID_W).astype(F32)
    inv_freq = ROPE_BASE ** (-jnp.arange(0, ROPE_AXIS_DIM, 2, dtype=F32) / ROPE_AXIS_DIM)
    ang_r = row[:, None] * inv_freq[None, :]
    ang_c = colp[:, None] * inv_freq[None, :]
    cos = jnp.concatenate([jnp.cos(ang_r)] * 2 + [jnp.cos(ang_c)] * 2, axis=-1)
    sin = jnp.concatenate([-jnp.sin(ang_r), jnp.sin(ang_r), -jnp.sin(ang_c), jnp.sin(ang_c)], axis=-1)
    return cos, sin


def _rope(x, cos, sin):
    half = ROPE_AXIS_DIM // 2
    lane = lax.broadcasted_iota(jnp.int32, x.shape, 1)
    partner = jnp.where(lane % ROPE_AXIS_DIM < half,
                        pltpu.roll(x, HEAD_DIM - half, axis=1), pltpu.roll(x, half, axis=1))
    return x * cos + partner * sin


def _attn_kernel(q_ref, k_ref, v_ref, cosq_ref, sinq_ref, cosk_ref, sink_ref, qg_ref, kg_ref,
                 o_ref, ks_ref, vs_ref):
    @pl.when(pl.program_id(2) == 0)
    def _():
        kn = _rms_scale(k_ref[...]) * kg_ref[...]
        ks_ref[...] = _rope(kn, cosk_ref[...], sink_ref[...]).astype(BF16)
        vs_ref[...] = v_ref[...].astype(BF16)

    scale = HEAD_DIM ** -0.5
    keys = ks_ref[...]
    vals = vs_ref[...]
    for g in range(ATT_GROUP):
        lanes = slice(g * HEAD_DIM, (g + 1) * HEAD_DIM)
        qn = _rms_scale(q_ref[:, lanes]) * qg_ref[...]
        qr = _rope(qn, cosq_ref[...], sinq_ref[...]).astype(BF16)
        s = lax.dot_general(qr, keys, (((1,), (1,)), ((), ())), preferred_element_type=F32) * scale
        p = jnp.exp(s - jnp.max(s, axis=-1, keepdims=True))
        denom = jnp.sum(p, axis=-1, keepdims=True)
        pv = jnp.dot(p.astype(BF16), vals, preferred_element_type=F32)
        o_ref[:, lanes] = (pv / denom).astype(o_ref.dtype)


def _attention(proj, seq_len, q_gain, k_gain, cos, sin):
    t = proj.shape[0]
    tq = min(ATT_Q_TILE, seq_len)
    nq = seq_len // tq
    qw = ATT_GROUP * HEAD_DIM
    kv = lambda off: pl.BlockSpec((seq_len, HEAD_DIM), lambda s, h, i, off=off: (s, off // HEAD_DIM + h))
    gain = pl.BlockSpec((1, HEAD_DIM), lambda s, h, i: (0, 0))
    table_q = pl.BlockSpec((tq, HEAD_DIM), lambda s, h, i: (i, 0))
    table_k = pl.BlockSpec((seq_len, HEAD_DIM), lambda s, h, i: (0, 0))
    return pl.pallas_call(
        _attn_kernel,
        grid=(t // seq_len, ATT_KV_HEADS, nq),
        in_specs=[pl.BlockSpec((tq, qw), lambda s, h, i: (s * nq + i, OFF_ATT_Q // qw + h)),
                  kv(OFF_ATT_K), kv(OFF_ATT_V), table_q, table_q, table_k, table_k, gain, gain],
        out_specs=pl.BlockSpec((tq, qw), lambda s, h, i: (s * nq + i, h)),
        out_shape=jax.ShapeDtypeStruct((t, ATT_WIDTH), BF16),
        scratch_shapes=[pltpu.VMEM((seq_len, HEAD_DIM), BF16), pltpu.VMEM((seq_len, HEAD_DIM), BF16)],
        compiler_params=_params(("parallel", "parallel", "arbitrary"), 40),
        name="attention",
    )(proj, proj, proj, cos, sin, cos, sin,
      q_gain.reshape(1, HEAD_DIM), k_gain.reshape(1, HEAD_DIM))


def _merge_kernel(h_ref, ys_ref, yh_ref, ya_ref, g0_ref, g1_ref, g2_ref,
                  ws_ref, wh_ref, wa_ref, wo_ref, o_ref, acc_ref):
    c = pl.program_id(1)

    @pl.when(c == 0)
    def _():
        acc_ref[...] = jnp.zeros_like(acc_ref)

    merged = (jax.nn.sigmoid(g0_ref[...]) * jnp.dot(ys_ref[...], ws_ref[...], preferred_element_type=F32)
              + jax.nn.sigmoid(g1_ref[...]) * jnp.dot(yh_ref[...], wh_ref[...], preferred_element_type=F32)
              + jax.nn.sigmoid(g2_ref[...]) * jnp.dot(ya_ref[...], wa_ref[...], preferred_element_type=F32))
    acc_ref[...] += jnp.dot(merged.astype(BF16), wo_ref[...], preferred_element_type=F32)

    @pl.when(c == pl.num_programs(1) - 1)
    def _():
        o_ref[...] = h_ref[...] + acc_ref[...]


def _merge(h, proj, y_ssm, y_hg, y_att, w_branch, w_out):
    t, d = h.shape
    tm = min(TOKEN_TILE, t)
    nc = d // MERGE_TILE
    gate = lambda b: pl.BlockSpec(
        (tm, MERGE_TILE), lambda i, c, b=b: (i, (OFF_GATE + b * D_MODEL) // MERGE_TILE + c))
    rows = lambda first, width: pl.BlockSpec(
        (width, MERGE_TILE), lambda i, c, first=first, width=width: (first // width, c))
    return pl.pallas_call(
        _merge_kernel,
        grid=(t // tm, nc),
        in_specs=[
            pl.BlockSpec((tm, d), lambda i, c: (i, 0)),
            pl.BlockSpec((tm, SSM_WIDTH), lambda i, c: (i, 0)),
            pl.BlockSpec((tm, HG_WIDTH), lambda i, c: (i, 0)),
            pl.BlockSpec((tm, ATT_WIDTH), lambda i, c: (i, 0)),
            gate(0), gate(1), gate(2),
            rows(0, SSM_WIDTH), rows(SSM_WIDTH, HG_WIDTH), rows(SSM_WIDTH + HG_WIDTH, ATT_WIDTH),
            pl.BlockSpec((MERGE_TILE, d), lambda i, c: (c, 0)),
        ],
        out_specs=pl.BlockSpec((tm, d), lambda i, c: (i, 0)),
        out_shape=jax.ShapeDtypeStruct((t, d), F32),
        scratch_shapes=[pltpu.VMEM((tm, d), F32)],
        compiler_params=_params(("parallel", "arbitrary"), 48),
        name="merge",
    )(h, y_ssm, y_hg, y_att, proj, proj, proj, w_branch, w_branch, w_branch, w_out)


def _trunk(x, p):
    n_seq, seq_len, d = x.shape
    cos, sin = _rope_tables(seq_len)
    h = x.reshape(n_seq * seq_len, d)
    for layer in range(DEPTH):
        h = _ffn(h, p["ffn1_norm"][layer], p["ffn1_w_gate_up"][layer], p["ffn1_w_down"][layer])
        proj = _in_proj(h, p["mix_norm"][layer], p["w_in"][layer])
        y_ssm = _s5(proj, seq_len, p["s5_a"][layer], p["s5_bbar"][layer], p["s5_c"][layer],
                    p["ssm_d"][layer], p["ssm_w_glu"][layer])
        y_hg = _hgrn(proj, seq_len, p["hg_lb"][layer], p["hg_out_norm"][layer])
        y_att = _attention(proj, seq_len, p["att_q_norm"][layer], p["att_k_norm"][layer], cos, sin)
        h = _merge(h, proj, y_ssm, y_hg, y_att, p["w_branch"][layer], p["w_out"][layer])
        h = _ffn(h, p["ffn2_norm"][layer], p["ffn2_w_gate_up"][layer], p["ffn2_w_down"][layer])
    return h.reshape(n_seq, seq_len, d)


def kernel(x_prompt, x_sample, ffn1_norm, ffn1_w_gate_up, ffn1_w_down, mix_norm, w_in, ssm_lambda_re, ssm_lambda_im, ssm_log_dt, ssm_b_re, ssm_b_im, ssm_c_re, ssm_c_im, ssm_d, ssm_w_glu, hg_lb_logits, hg_out_norm, att_q_norm, att_k_norm, w_branch, w_out, ffn2_norm, ffn2_w_gate_up, ffn2_w_down):
    s5_a, s5_bbar, s5_c = _s5_prepare(ssm_lambda_re, ssm_lambda_im, ssm_log_dt,
                                      ssm_b_re, ssm_b_im, ssm_c_re, ssm_c_im)
    f32 = lambda a: a.astype(F32)
    bf16 = lambda a: a.astype(BF16)
    p = dict(
        ffn1_norm=f32(ffn1_norm), ffn1_w_gate_up=bf16(ffn1_w_gate_up), ffn1_w_down=bf16(ffn1_w_down),
        mix_norm=f32(mix_norm), w_in=bf16(w_in),
        s5_a=s5_a, s5_bbar=s5_bbar, s5_c=s5_c, ssm_d=f32(ssm_d), ssm_w_glu=bf16(ssm_w_glu),
        hg_lb=_hg_lower_bound(hg_lb_logits), hg_out_norm=f32(hg_out_norm),
        att_q_norm=f32(att_q_norm), att_k_norm=f32(att_k_norm),
        w_branch=bf16(w_branch), w_out=bf16(w_out),
        ffn2_norm=f32(ffn2_norm), ffn2_w_gate_up=bf16(ffn2_w_gate_up), ffn2_w_down=bf16(ffn2_w_down),
    )
    return _trunk(x_prompt, p), _trunk(x_sample, p)
```

```python
import functools
import math

import jax
import jax.numpy as jnp
from jax import lax
from jax.experimental import pallas as pl
from jax.experimental.pallas import tpu as pltpu

F32 = jnp.float32
BF16 = jnp.bfloat16

D_MODEL = 2048
DEPTH = 4
GRID_W = 64
EPS = 1e-6
LB_FLOOR = 1e-30
SSM_WIDTH = 512
SSM_GROUP = 16
SSM_GROUPS = 32
SSM_STATE = 64
SSM_STATES = SSM_GROUPS * SSM_STATE
HG_HEAD_DIM = 128
HG_WIDTH = 512
HG_HEADS = 4
HEAD_DIM = 128
ATT_Q_HEADS = 8
ATT_KV_HEADS = 2
ATT_GROUP = ATT_Q_HEADS // ATT_KV_HEADS
ATT_WIDTH = 1024
KV_WIDTH = 256
ROPE_BASE = 10000.0
ROPE_AXIS_DIM = HEAD_DIM // 2
N_BRANCH = 3
D_FF = 5632
GATE_WIDTH = N_BRANCH * D_MODEL
MAIN_WIDTH = SSM_WIDTH + 5 * HG_WIDTH + ATT_WIDTH + 2 * KV_WIDTH

OFF_SSM = 0
OFF_HG_Q = 512
OFF_HG_FF = 1024
OFF_HG_FB = 1536
OFF_HG_I = 2048
OFF_HG_G = 2560
OFF_ATT_Q = 3072
OFF_ATT_K = 4096
OFF_ATT_V = 4352

LANES = 128
SLABS = SSM_STATES // LANES
S5_HALVES = 2
HALF_WIDTH = SSM_WIDTH // S5_HALVES
HALF_STATES = SSM_STATES // S5_HALVES
HALF_SLABS = SLABS // S5_HALVES

TOKEN_TILE = 512
FF_TILE = 512
PROJ_ROWS = 1024
PROJ_COLS = 768
MERGE_TILE = 512
ATT_Q_TILE = 256
S5_CHUNK = 256
S5_PITCH = S5_CHUNK + 8
HG_CHUNK = 128
HG_SUB = 16
HG_HEADS_PER_STEP = 2

MIB = 1024 * 1024


def _params(semantics, vmem_mib):
    return pltpu.CompilerParams(dimension_semantics=semantics, vmem_limit_bytes=vmem_mib * MIB)


def _resident(shape, index_map):
    return pl.BlockSpec(shape, index_map, pipeline_mode=pl.Buffered(1))


def _rms_scale(x):
    return x * lax.rsqrt(jnp.mean(x * x, axis=-1, keepdims=True) + EPS)


def _nt_dot(a, b):
    return lax.dot_general(a, b, (((1,), (1,)), ((), ())), preferred_element_type=F32)


def _ffn_kernel(x_ref, g_ref, wg_ref, wu_ref, wd_ref, *rest, emit_normed):
    if emit_normed:
        g2_ref, o_ref, u_ref, xn_ref, acc_ref = rest
    else:
        o_ref, xn_ref, acc_ref = rest
    f = pl.program_id(1)

    @pl.when(f == 0)
    def _():
        xn_ref[...] = (_rms_scale(x_ref[...]) * g_ref[...]).astype(BF16)
        acc_ref[...] = jnp.zeros_like(acc_ref)

    xn = xn_ref[...]
    gate = jnp.dot(xn, wg_ref[...], preferred_element_type=F32)
    up = jnp.dot(xn, wu_ref[...], preferred_element_type=F32)
    act = (gate * jax.nn.sigmoid(gate) * up).astype(BF16)
    acc_ref[...] += jnp.dot(act, wd_ref[...], preferred_element_type=F32)

    @pl.when(f == pl.num_programs(1) - 1)
    def _():
        out = x_ref[...] + 0.5 * acc_ref[...]
        o_ref[...] = out
        if emit_normed:
            u_ref[...] = (_rms_scale(out) * g2_ref[...]).astype(BF16)


def _ffn(x, layer, gain, w_gate_up, w_down, next_gain=None):
    t, d = x.shape
    tm = min(TOKEN_TILE, t)
    nf = D_FF // FF_TILE
    emit = next_gain is not None
    gain_spec = pl.BlockSpec((None, 1, d), lambda i, f: (layer, 0, 0))
    in_specs = [
        pl.BlockSpec((tm, d), lambda i, f: (i, 0)),
        gain_spec,
        pl.BlockSpec((None, d, FF_TILE), lambda i, f: (layer, 0, f)),
        pl.BlockSpec((None, d, FF_TILE), lambda i, f: (layer, 0, f + nf)),
        pl.BlockSpec((None, FF_TILE, d), lambda i, f: (layer, f, 0)),
    ]
    args = [x, gain, w_gate_up, w_gate_up, w_down]
    out_specs = [pl.BlockSpec((tm, d), lambda i, f: (i, 0))]
    out_shape = [jax.ShapeDtypeStruct((t, d), F32)]
    if emit:
        in_specs.append(gain_spec)
        args.append(next_gain)
        out_specs.append(pl.BlockSpec((tm, d), lambda i, f: (i, 0)))
        out_shape.append(jax.ShapeDtypeStruct((t, d), BF16))
    outs = pl.pallas_call(
        functools.partial(_ffn_kernel, emit_normed=emit),
        grid=(t // tm, nf),
        in_specs=in_specs,
        out_specs=out_specs,
        out_shape=out_shape,
        scratch_shapes=[pltpu.VMEM((tm, d), BF16), pltpu.VMEM((tm, d), F32)],
        compiler_params=_params(("parallel", "arbitrary"), 52),
        name="ffn_norm" if emit else "ffn",
    )(*args)
    return outs if emit else outs[0]


def _proj_kernel(a_ref, w_ref, o_ref, *, squash):
    acc = jnp.dot(a_ref[...], w_ref[...], preferred_element_type=F32)
    o_ref[...] = (jax.nn.sigmoid(acc) if squash else acc).astype(o_ref.dtype)


def _in_proj(u, layer, w_in, first_col, width, squash):
    t, d = u.shape
    tm = min(PROJ_ROWS, t)
    col0 = first_col // PROJ_COLS
    return pl.pallas_call(
        functools.partial(_proj_kernel, squash=squash),
        grid=(t // tm, width // PROJ_COLS),
        in_specs=[pl.BlockSpec((tm, d), lambda i, j: (i, 0)),
                  pl.BlockSpec((None, d, PROJ_COLS), lambda i, j: (layer, 0, col0 + j))],
        out_specs=pl.BlockSpec((tm, PROJ_COLS), lambda i, j: (i, j)),
        out_shape=jax.ShapeDtypeStruct((t, width), BF16 if squash else F32),
        compiler_params=_params(("parallel", "arbitrary"), 40),
        name="gate_proj" if squash else "in_proj",
    )(u, w_in)


def _lower_bound_kernel(logit_ref, lb_ref):
    x = logit_ref[...]
    e = jnp.exp(x - jnp.max(x, axis=0, keepdims=True))
    p = e / jnp.sum(e, axis=0, keepdims=True)
    run = jnp.zeros_like(p[0:1])
    for layer in range(DEPTH):
        run = run + p[layer:layer + 1]
        lb_ref[layer:layer + 1, :] = jnp.clip(run - p[0:1], 0.0, 1.0 - 1e-6)


def _hg_lower_bound(logits):
    flat = logits.astype(F32).reshape(DEPTH, 2 * HG_WIDTH)
    lb = pl.pallas_call(
        _lower_bound_kernel,
        out_shape=jax.ShapeDtypeStruct(flat.shape, F32),
        name="hg_lower_bound",
    )(flat)
    return lb.reshape(DEPTH, 2, HG_WIDTH)


def _s5_prep_kernel(lre_ref, lim_ref, ldt_ref, bre_ref, bim_ref, a_ref, bbar_ref):
    lre = lre_ref[0]
    lim = lim_ref[0]
    dt = jnp.exp(ldt_ref[0])
    mag = jnp.exp(lre * dt)
    a_re = mag * jnp.cos(lim * dt)
    a_im = mag * jnp.sin(lim * dt)
    den = lre * lre + lim * lim
    num_re = a_re - 1.0
    coef_re = (num_re * lre + a_im * lim) / den
    coef_im = (a_im * lre - num_re * lim) / den
    a_ref[0, 0:1, :] = a_re
    a_ref[0, 1:2, :] = a_im
    bre = bre_ref[0]
    bim = bim_ref[0]
    bbar_ref[0, :, 0:HALF_STATES] = (coef_re * bre - coef_im * bim).astype(BF16)
    bbar_ref[0, :, HALF_STATES:] = (coef_re * bim + coef_im * bre).astype(BF16)


def _block_diag_in(b):
    n = b.shape[0]
    per = SSM_GROUPS // S5_HALVES
    b = b.reshape(n * S5_HALVES, per, SSM_STATE, SSM_GROUP)
    full = jnp.einsum('xgnp,gh->xgphn', b, jnp.eye(per, dtype=b.dtype))
    return full.reshape(n * S5_HALVES, HALF_WIDTH, HALF_STATES)


def _block_diag_out(c):
    n = c.shape[0]
    per = SSM_GROUPS // S5_HALVES
    c = c.reshape(n * S5_HALVES, per, SSM_GROUP, SSM_STATE)
    full = jnp.einsum('xgpn,gh->xgnhp', c, jnp.eye(per, dtype=c.dtype))
    return full.reshape(n * S5_HALVES, HALF_STATES, HALF_WIDTH)


def _s5_prepare(lam_re, lam_im, log_dt, b_re, b_im, c_re, c_im):
    n = DEPTH * 2
    m = n * S5_HALVES
    flat = lambda a: a.astype(F32).reshape(m, 1, HALF_STATES)
    ldt = jnp.repeat(log_dt.astype(F32).reshape(n, SSM_GROUPS), SSM_STATE, axis=-1)
    bre = _block_diag_in(b_re.astype(F32).reshape(n, SSM_GROUPS, SSM_STATE, SSM_GROUP))
    bim = _block_diag_in(b_im.astype(F32).reshape(n, SSM_GROUPS, SSM_STATE, SSM_GROUP))
    vec = pl.BlockSpec((1, 1, HALF_STATES), lambda i: (i, 0, 0))
    mat = pl.BlockSpec((1, HALF_WIDTH, HALF_STATES), lambda i: (i, 0, 0))
    a, bbar = pl.pallas_call(
        _s5_prep_kernel,
        grid=(m,),
        in_specs=[vec, vec, vec, mat, mat],
        out_specs=[pl.BlockSpec((1, 2, HALF_STATES), lambda i: (i, 0, 0)),
                   pl.BlockSpec((1, HALF_WIDTH, 2 * HALF_STATES), lambda i: (i, 0, 0))],
        out_shape=[jax.ShapeDtypeStruct((m, 2, HALF_STATES), F32),
                   jax.ShapeDtypeStruct((m, HALF_WIDTH, 2 * HALF_STATES), BF16)],
        compiler_params=_params(("parallel",), 32),
        name="s5_prep",
    )(flat(lam_re), flat(lam_im), flat(ldt), bre, bim)
    a = a.reshape(DEPTH, 2, S5_HALVES, 2, HALF_STATES).transpose(0, 1, 3, 2, 4)
    a = a.reshape(DEPTH, 2, 2, SLABS, LANES)
    bbar = bbar.reshape(DEPTH, 2, S5_HALVES, HALF_WIDTH, 2 * HALF_STATES)
    cre = _block_diag_out(c_re.astype(F32).reshape(n, SSM_GROUPS, SSM_GROUP, SSM_STATE))
    cim = _block_diag_out(c_im.astype(F32).reshape(n, SSM_GROUPS, SSM_GROUP, SSM_STATE))
    cmat = jnp.stack([cre.reshape(DEPTH, 2, S5_HALVES, HALF_STATES, HALF_WIDTH),
                      cim.reshape(DEPTH, 2, S5_HALVES, HALF_STATES, HALF_WIDTH)], axis=2)
    return a, bbar, cmat.astype(BF16)


def _s5_kernel(u_ref, a_ref, bbar_ref, c_ref, d_ref, wglu_ref, o_ref, y_ref, sf_ref, sb_ref):
    seq_len = u_ref.shape[0]
    q = min(S5_CHUNK, seq_len)
    n_chunks = seq_len // q
    im0 = SLABS * S5_PITCH
    y_ref[...] = d_ref[...] * u_ref[...]
    a_f_re, a_f_im = a_ref[0, 0], a_ref[0, 1]
    a_b_re, a_b_im = a_ref[1, 0], a_ref[1, 1]

    def slab_rows(j):
        return slice(j * S5_PITCH, j * S5_PITCH + q)

    def project_in(rows, direction, slab_ref):
        for half in range(S5_HALVES):
            u_half = u_ref[rows, half * HALF_WIDTH:(half + 1) * HALF_WIDTH].astype(BF16)
            bu = jnp.dot(u_half, bbar_ref[direction, half], preferred_element_type=F32)
            for part in range(2):
                for j in range(HALF_SLABS):
                    col = part * HALF_STATES + j * LANES
                    slab_ref[slab_rows(part * SLABS + half * HALF_SLABS + j), :] = bu[:, col:col + LANES]

    def project_out(direction, slab_ref):
        halves = []
        for half in range(S5_HALVES):
            def natural(part):
                first = part * SLABS + half * HALF_SLABS
                return jnp.concatenate([slab_ref[slab_rows(first + j), :] for j in range(HALF_SLABS)],
                                       axis=1).astype(BF16)
            halves.append(jnp.dot(natural(0), c_ref[direction, 0, half], preferred_element_type=F32)
                          - jnp.dot(natural(1), c_ref[direction, 1, half], preferred_element_type=F32))
        return jnp.concatenate(halves, axis=1)

    def advance(slab_ref, t, a_re, a_im, x_re, x_im):
        re_rows = pl.ds(t, SLABS, stride=S5_PITCH)
        im_rows = pl.ds(im0 + t, SLABS, stride=S5_PITCH)
        n_re = a_re * x_re - a_im * x_im + slab_ref[re_rows, :]
        n_im = a_re * x_im + a_im * x_re + slab_ref[im_rows, :]
        slab_ref[re_rows, :] = n_re
        slab_ref[im_rows, :] = n_im
        return n_re, n_im

    def chunk(c, carry):
        rows_f = pl.ds(pl.multiple_of(c * q, q), q)
        rows_b = pl.ds(pl.multiple_of((n_chunks - 1 - c) * q, q), q)
        project_in(rows_f, 0, sf_ref)
        project_in(rows_b, 1, sb_ref)

        def step(i, st):
            f_re, f_im = advance(sf_ref, i, a_f_re, a_f_im, st[0], st[1])
            b_re, b_im = advance(sb_ref, q - 1 - i, a_b_re, a_b_im, st[2], st[3])
            return f_re, f_im, b_re, b_im

        carry = lax.fori_loop(0, q, step, carry, unroll=4)
        y_ref[rows_f, :] += project_out(0, sf_ref)
        y_ref[rows_b, :] += project_out(1, sb_ref)
        return carry

    zero = jnp.zeros((SLABS, LANES), F32)
    lax.fori_loop(0, n_chunks, chunk, (zero, zero, zero, zero))
    z = jax.nn.gelu(y_ref[...])
    glu = jnp.dot(z.astype(BF16), wglu_ref[...], preferred_element_type=F32)
    o_ref[...] = (z * jax.nn.sigmoid(glu)).astype(o_ref.dtype)


def _s5(proj, seq_len, layer, a, bbar, cmat, d_skip, w_glu):
    t = proj.shape[0]

    def whole(shape):
        return _resident((None,) + shape, lambda s: (layer,) + (0,) * len(shape))

    return pl.pallas_call(
        _s5_kernel,
        grid=(t // seq_len,),
        in_specs=[
            pl.BlockSpec((seq_len, SSM_WIDTH), lambda s: (s, OFF_SSM // SSM_WIDTH)),
            whole((2, 2, SLABS, LANES)),
            whole((2, S5_HALVES, HALF_WIDTH, 2 * HALF_STATES)),
            whole((2, 2, S5_HALVES, HALF_STATES, HALF_WIDTH)),
            whole((1, SSM_WIDTH)),
            whole((SSM_WIDTH, SSM_WIDTH)),
        ],
        out_specs=pl.BlockSpec((seq_len, SSM_WIDTH), lambda s: (s, 0)),
        out_shape=jax.ShapeDtypeStruct((t, SSM_WIDTH), BF16),
        scratch_shapes=[pltpu.VMEM((seq_len, SSM_WIDTH), F32),
                        pltpu.VMEM((2 * SLABS * S5_PITCH, LANES), F32),
                        pltpu.VMEM((2 * SLABS * S5_PITCH, LANES), F32)],
        compiler_params=_params(("parallel",), 48),
        name="s5",
    )(proj, a, bbar, cmat, d_skip, w_glu)


def _hg_gates(z, lb):
    e = jnp.exp(-jnp.abs(z))
    r = 1.0 / (1.0 + e)
    er = e * r
    pos = z >= 0.0
    log_f = jnp.log(jnp.maximum(lb, LB_FLOOR) + (1.0 - lb) * jnp.where(pos, r, er))
    return log_f, (1.0 - lb) * jnp.where(pos, er, r)


def _pair_diag(x):
    lane = lax.broadcasted_iota(jnp.int32, x.shape, 1)
    zero = jnp.zeros_like(x)
    return jnp.concatenate([jnp.where(lane < HG_HEAD_DIM, x, zero),
                            jnp.where(lane >= HG_HEAD_DIM, x, zero)], axis=0)


def _hg_direction(q_raw, z, vc, lb, state_t, reverse):
    c = q_raw.shape[0]
    qc = q_raw * jax.nn.sigmoid(q_raw)
    lfc, kc = _hg_gates(z, lb)
    row = lax.broadcasted_iota(jnp.int32, (c, c), 0)
    col = lax.broadcasted_iota(jnp.int32, (c, c), 1)
    earlier = (col >= row) if reverse else (col <= row)
    ones = jnp.where(earlier, 1.0, 0.0).astype(BF16)
    part0 = lfc.astype(BF16)
    rest = lfc - part0.astype(F32)
    part1 = rest.astype(BF16)
    part2 = (rest - part1.astype(F32)).astype(BF16)
    sums = jnp.dot(ones, jnp.concatenate([part0, part1, part2], axis=1), preferred_element_type=F32)
    w = lfc.shape[1]
    cum = sums[:, 0:w] + sums[:, w:2 * w] + sums[:, 2 * w:3 * w]
    total = cum[0:1] if reverse else cum[c - 1:c]
    vb = vc.astype(BF16)

    out = _nt_dot((qc * jnp.exp(cum)).astype(BF16), _pair_diag(state_t.astype(BF16)))
    k_state = (kc * jnp.exp(total - cum)).astype(BF16)
    v_t = jnp.concatenate([vc[:, 0:HG_HEAD_DIM].T, vc[:, HG_HEAD_DIM:].T], axis=1).astype(BF16)
    new_state = state_t * jnp.exp(total) + jnp.dot(v_t, _pair_diag(k_state), preferred_element_type=F32)
    return dict(qc=qc, kc=kc, cum=cum, cum_before=cum - lfc, vb=vb, earlier=earlier,
                out=out, state=new_state, reverse=reverse, keys=None, prev_ref=None)


def _hg_block_operands(d, step):
    c = d["qc"].shape[0]
    n_blk = c // HG_SUB
    reverse = d["reverse"]
    blk = n_blk - 1 - step if reverse else step
    lo, hi = blk * HG_SUB, (blk + 1) * HG_SUB
    cum = d["cum"]
    ref_row = d["cum_before"][hi - 1:hi] if reverse else d["cum_before"][lo:lo + 1]
    fresh = d["kc"][lo:hi] * jnp.exp(ref_row - cum[lo:hi])
    if d["keys"] is None:
        keys = fresh
    else:
        moved = d["keys"] * jnp.exp(ref_row - d["prev_ref"])
        keys = jnp.concatenate([fresh, moved] if reverse else [moved, fresh], axis=0)
    d["keys"], d["prev_ref"] = keys, ref_row
    padded = keys
    if keys.shape[0] < c:
        blank = jnp.zeros((c - keys.shape[0], keys.shape[1]), F32)
        padded = jnp.concatenate([blank, keys] if reverse else [keys, blank], axis=0)
    q_blk = (d["qc"][lo:hi] * jnp.exp(cum[lo:hi] - ref_row)).astype(BF16)
    return blk, _pair_diag(q_blk), padded.astype(BF16)


def _hg_chunk_pair(fwd, bwd):
    c = fwd["qc"].shape[0]
    n_blk = c // HG_SUB
    rows = {id(fwd): [[None] * n_blk, [None] * n_blk], id(bwd): [[None] * n_blk, [None] * n_blk]}
    for step in range(n_blk):
        blk_f, q_f, k_f = _hg_block_operands(fwd, step)
        blk_b, q_b, k_b = _hg_block_operands(bwd, step)
        res = _nt_dot(jnp.concatenate([q_f, q_b], axis=0), jnp.concatenate([k_f, k_b], axis=0))
        for head in range(HG_HEADS_PER_STEP):
            rows[id(fwd)][head][blk_f] = res[head * HG_SUB:(head + 1) * HG_SUB, 0:c]
            rows[id(bwd)][head][blk_b] = res[(2 + head) * HG_SUB:(3 + head) * HG_SUB, c:2 * c]
    outs = []
    for d in (fwd, bwd):
        scores = jnp.concatenate(
            [jnp.where(d["earlier"], jnp.concatenate(rows[id(d)][head], axis=0), 0.0).astype(BF16)
             for head in range(HG_HEADS_PER_STEP)], axis=1)
        outs.append(d["out"] + jnp.dot(scores, _pair_diag(d["vb"]), preferred_element_type=F32))
    return outs


def _hgrn_kernel(q_ref, ff_ref, fb_ref, i_ref, g_ref, lb_ref, gain_ref, o_ref, acc_ref):
    seq_len = q_ref.shape[0]
    c = HG_CHUNK
    n_chunks = seq_len // c
    acc_ref[...] = jnp.zeros_like(acc_ref)

    def chunk(ci, states):
        rows_f = pl.ds(pl.multiple_of(ci * c, c), c)
        rows_b = pl.ds(pl.multiple_of((n_chunks - 1 - ci) * c, c), c)
        fwd = _hg_direction(q_ref[rows_f, :], ff_ref[rows_f, :], i_ref[rows_f, :], lb_ref[0:1, :],
                            states[0], False)
        bwd = _hg_direction(q_ref[rows_b, :], fb_ref[rows_b, :], i_ref[rows_b, :], lb_ref[1:2, :],
                            states[1], True)
        out_f, out_b = _hg_chunk_pair(fwd, bwd)
        acc_ref[rows_f, :] += out_f
        acc_ref[rows_b, :] += out_b
        return fwd["state"], bwd["state"]

    zero = jnp.zeros((HG_HEAD_DIM, HG_HEADS_PER_STEP * HG_HEAD_DIM), F32)
    lax.fori_loop(0, n_chunks, chunk, (zero, zero))
    for head in range(HG_HEADS_PER_STEP):
        lanes = slice(head * HG_HEAD_DIM, (head + 1) * HG_HEAD_DIM)
        g_raw = g_ref[:, lanes]
        normed = _rms_scale(acc_ref[:, lanes]) * gain_ref[...]
        o_ref[:, lanes] = (normed * (g_raw * jax.nn.sigmoid(g_raw))).astype(o_ref.dtype)


def _hgrn(proj, seq_len, layer, lower_bound, out_gain):
    t = proj.shape[0]
    width = HG_HEADS_PER_STEP * HG_HEAD_DIM
    col = lambda off: pl.BlockSpec((seq_len, width), lambda s, h, off=off: (s, off // width + h))
    return pl.pallas_call(
        _hgrn_kernel,
        grid=(t // seq_len, HG_HEADS // HG_HEADS_PER_STEP),
        in_specs=[col(OFF_HG_Q), col(OFF_HG_FF), col(OFF_HG_FB), col(OFF_HG_I), col(OFF_HG_G),
                  pl.BlockSpec((None, 2, width), lambda s, h: (layer, 0, h)),
                  pl.BlockSpec((None, 1, HG_HEAD_DIM), lambda s, h: (layer, 0, 0))],
        out_specs=pl.BlockSpec((seq_len, width), lambda s, h: (s, h)),
        out_shape=jax.ShapeDtypeStruct((t, HG_WIDTH), BF16),
        scratch_shapes=[pltpu.VMEM((seq_len, width), F32)],
        compiler_params=_params(("parallel", "parallel"), 40),
        name="hgrn2",
    )(proj, proj, proj, proj, proj, lower_bound, out_gain)


def _rope_tables(seq_len):
    pos = jnp.arange(seq_len, dtype=jnp.int32)
    row = (pos // GRID_W).astype(F32)
    colp = (pos % GRID_W).astype(F32)
    inv_freq = ROPE_BASE ** (-jnp.arange(0, ROPE_AXIS_DIM, 2, dtype=F32) / ROPE_AXIS_DIM)
    ang_r = row[:, None] * inv_freq[None, :]
    ang_c = colp[:, None] * inv_freq[None, :]
    cos = jnp.concatenate([jnp.cos(ang_r)] * 2 + [jnp.cos(ang_c)] * 2, axis=-1)
    sin = jnp.concatenate([-jnp.sin(ang_r), jnp.sin(ang_r), -jnp.sin(ang_c), jnp.sin(ang_c)], axis=-1)
    return cos, sin


def _rope(x, cos, sin):
    half = ROPE_AXIS_DIM // 2
    lane = lax.broadcasted_iota(jnp.int32, x.shape, 1)
    partner = jnp.where(lane % ROPE_AXIS_DIM < half,
                        pltpu.roll(x, HEAD_DIM - half, axis=1), pltpu.roll(x, half, axis=1))
    return x * cos + partner * sin


def _attn_kernel(q_ref, k_ref, v_ref, cosq_ref, sinq_ref, cosk_ref, sink_ref, qg_ref, kg_ref,
                 o_ref, ks_ref, vs_ref):
    @pl.when(pl.program_id(2) == 0)
    def _():
        kn = _rms_scale(k_ref[...]) * kg_ref[...]
        ks_ref[...] = _rope(kn, cosk_ref[...], sink_ref[...]).astype(BF16)
        vs_ref[...] = v_ref[...].astype(BF16)

    scale = HEAD_DIM ** -0.5 * math.log2(math.e)
    keys = ks_ref[...]
    vals = vs_ref[...]
    for g in range(ATT_GROUP):
        lanes = slice(g * HEAD_DIM, (g + 1) * HEAD_DIM)
        qn = _rms_scale(q_ref[:, lanes]) * qg_ref[...]
        qr = (_rope(qn, cosq_ref[...], sinq_ref[...]) * scale).astype(BF16)
        s = _nt_dot(qr, keys)
        p = jnp.exp2(s - jnp.max(s, axis=-1, keepdims=True))
        denom = jnp.sum(p, axis=-1, keepdims=True)
        pv = jnp.dot(p.astype(BF16), vals, preferred_element_type=F32)
        o_ref[:, lanes] = (pv / denom).astype(o_ref.dtype)


def _attention(proj, seq_len, layer, q_gain, k_gain, cos, sin):
    t = proj.shape[0]
    tq = min(ATT_Q_TILE, seq_len)
    nq = seq_len // tq
    qw = ATT_GROUP * HEAD_DIM
    kv = lambda off: pl.BlockSpec((seq_len, HEAD_DIM), lambda s, h, i, off=off: (s, off // HEAD_DIM + h))
    gain = pl.BlockSpec((None, 1, HEAD_DIM), lambda s, h, i: (layer, 0, 0))
    table_q = pl.BlockSpec((tq, HEAD_DIM), lambda s, h, i: (i, 0))
    table_k = pl.BlockSpec((seq_len, HEAD_DIM), lambda s, h, i: (0, 0))
    return pl.pallas_call(
        _attn_kernel,
        grid=(t // seq_len, ATT_KV_HEADS, nq),
        in_specs=[pl.BlockSpec((tq, qw), lambda s, h, i: (s * nq + i, OFF_ATT_Q // qw + h)),
                  kv(OFF_ATT_K), kv(OFF_ATT_V), table_q, table_q, table_k, table_k, gain, gain],
        out_specs=pl.BlockSpec((tq, qw), lambda s, h, i: (s * nq + i, h)),
        out_shape=jax.ShapeDtypeStruct((t, ATT_WIDTH), BF16),
        scratch_shapes=[pltpu.VMEM((seq_len, HEAD_DIM), BF16), pltpu.VMEM((seq_len, HEAD_DIM), BF16)],
        compiler_params=_params(("parallel", "parallel", "arbitrary"), 40),
        name="attention",
    )(proj, proj, proj, cos, sin, cos, sin, q_gain, k_gain)


def _merge_kernel(h_ref, ys_ref, yh_ref, ya_ref, g_ref, wb_ref, wo_ref, o_ref):
    acc = h_ref[...]
    ys, yh, ya = ys_ref[...], yh_ref[...], ya_ref[...]
    hg0, att0 = SSM_WIDTH, SSM_WIDTH + HG_WIDTH
    for c in range(D_MODEL // MERGE_TILE):
        cols = slice(c * MERGE_TILE, (c + 1) * MERGE_TILE)
        gate = lambda b: g_ref[:, b * D_MODEL + c * MERGE_TILE:b * D_MODEL + (c + 1) * MERGE_TILE]
        merged = (gate(0) * jnp.dot(ys, wb_ref[0:hg0, cols], preferred_element_type=F32)
                  + gate(1) * jnp.dot(yh, wb_ref[hg0:att0, cols], preferred_element_type=F32)
                  + gate(2) * jnp.dot(ya, wb_ref[att0:, cols], preferred_element_type=F32))
        acc = acc + jnp.dot(merged.astype(BF16), wo_ref[cols, :], preferred_element_type=F32)
    o_ref[...] = acc


def _merge(h, gates, y_ssm, y_hg, y_att, layer, w_branch, w_out):
    t, d = h.shape
    tm = min(TOKEN_TILE, t)
    rows = lambda width: pl.BlockSpec((tm, width), lambda i: (i, 0))
    weight = _resident((None, d, d), lambda i: (layer, 0, 0))
    return pl.pallas_call(
        _merge_kernel,
        grid=(t // tm,),
        in_specs=[rows(d), rows(SSM_WIDTH), rows(HG_WIDTH), rows(ATT_WIDTH), rows(GATE_WIDTH),
                  weight, weight],
        out_specs=rows(d),
        out_shape=jax.ShapeDtypeStruct((t, d), F32),
        compiler_params=_params(("parallel",), 60),
        name="merge",
    )(h, y_ssm, y_hg, y_att, gates, w_branch, w_out)


def _trunk(x, p):
    n_seq, seq_len, d = x.shape
    cos, sin = _rope_tables(seq_len)
    h = x.reshape(n_seq * seq_len, d)
    for layer in range(DEPTH):
        h, u = _ffn(h, layer, p["ffn1_norm"], p["ffn1_w_gate_up"], p["ffn1_w_down"], p["mix_norm"])
        proj = _in_proj(u, layer, p["w_in"], 0, MAIN_WIDTH, False)
        gates = _in_proj(u, layer, p["w_in"], MAIN_WIDTH, GATE_WIDTH, True)
        y_ssm = _s5(proj, seq_len, layer, p["s5_a"], p["s5_bbar"], p["s5_c"], p["ssm_d"], p["ssm_w_glu"])
        y_hg = _hgrn(proj, seq_len, layer, p["hg_lb"], p["hg_out_norm"])
        y_att = _attention(proj, seq_len, layer, p["att_q_norm"], p["att_k_norm"], cos, sin)
        h = _merge(h, gates, y_ssm, y_hg, y_att, layer, p["w_branch"], p["w_out"])
        h = _ffn(h, layer, p["ffn2_norm"], p["ffn2_w_gate_up"], p["ffn2_w_down"])
    return h.reshape(n_seq, seq_len, d)


def kernel(x_prompt, x_sample, ffn1_norm, ffn1_w_gate_up, ffn1_w_down, mix_norm, w_in, ssm_lambda_re, ssm_lambda_im, ssm_log_dt, ssm_b_re, ssm_b_im, ssm_c_re, ssm_c_im, ssm_d, ssm_w_glu, hg_lb_logits, hg_out_norm, att_q_norm, att_k_norm, w_branch, w_out, ffn2_norm, ffn2_w_gate_up, ffn2_w_down):
    s5_a, s5_bbar, s5_c = _s5_prepare(ssm_lambda_re, ssm_lambda_im, ssm_log_dt,
                                      ssm_b_re, ssm_b_im, ssm_c_re, ssm_c_im)
    row = lambda a: a.astype(F32).reshape(DEPTH, 1, a.shape[-1])
    bf16 = lambda a: a.astype(BF16)
    p = dict(
        ffn1_norm=row(ffn1_norm), ffn1_w_gate_up=bf16(ffn1_w_gate_up), ffn1_w_down=bf16(ffn1_w_down),
        mix_norm=row(mix_norm), w_in=bf16(w_in),
        s5_a=s5_a, s5_bbar=s5_bbar, s5_c=s5_c, ssm_d=row(ssm_d), ssm_w_glu=bf16(ssm_w_glu),
        hg_lb=_hg_lower_bound(hg_lb_logits), hg_out_norm=row(hg_out_norm),
        att_q_norm=row(att_q_norm), att_k_norm=row(att_k_norm),
        w_branch=bf16(w_branch), w_out=bf16(w_out),
        ffn2_norm=row(ffn2_norm), ffn2_w_gate_up=bf16(ffn2_w_gate_up), ffn2_w_down=bf16(ffn2_w_down),
    )
    return _trunk(x_prompt, p), _trunk(x_sample, p)
```

```python
import functools
import math

import jax
import jax.numpy as jnp
from jax import lax
from jax.experimental import pallas as pl
from jax.experimental.pallas import tpu as pltpu

F32 = jnp.float32
BF16 = jnp.bfloat16

D_MODEL = 2048
DEPTH = 4
GRID_W = 64
EPS = 1e-6
LB_FLOOR = 1e-30
SSM_WIDTH = 512
SSM_GROUP = 16
SSM_GROUPS = 32
SSM_STATE = 64
SSM_STATES = SSM_GROUPS * SSM_STATE
HG_HEAD_DIM = 128
HG_WIDTH = 512
HG_HEADS = 4
HEAD_DIM = 128
ATT_Q_HEADS = 8
ATT_KV_HEADS = 2
ATT_GROUP = ATT_Q_HEADS // ATT_KV_HEADS
ATT_WIDTH = 1024
KV_WIDTH = 256
ROPE_BASE = 10000.0
ROPE_AXIS_DIM = HEAD_DIM // 2
N_BRANCH = 3
D_FF = 5632
GATE_WIDTH = N_BRANCH * D_MODEL
MAIN_WIDTH = SSM_WIDTH + 5 * HG_WIDTH + ATT_WIDTH + 2 * KV_WIDTH

OFF_SSM = 0
OFF_HG_Q = 512
OFF_HG_FF = 1024
OFF_HG_FB = 1536
OFF_HG_I = 2048
OFF_HG_G = 2560
OFF_ATT_Q = 3072
OFF_ATT_K = 4096
OFF_ATT_V = 4352

LANES = 128
SLABS = SSM_STATES // LANES
S5_HALVES = 2
HALF_WIDTH = SSM_WIDTH // S5_HALVES
HALF_STATES = SSM_STATES // S5_HALVES
HALF_SLABS = SLABS // S5_HALVES

TOKEN_TILE = 512
FF_TILE = 512
PROJ_ROWS = 1024
PROJ_COLS = 1536
MERGE_TILE = 512
ATT_Q_TILE = 256
S5_CHUNK = 256
S5_PITCH = S5_CHUNK + 8
HG_CHUNK = 128
HG_SUB = 16
HG_HEADS_PER_STEP = 2

MIB = 1024 * 1024


def _params(semantics, vmem_mib):
    return pltpu.CompilerParams(dimension_semantics=semantics, vmem_limit_bytes=vmem_mib * MIB)


def _resident(shape, index_map):
    return pl.BlockSpec(shape, index_map, pipeline_mode=pl.Buffered(1))


def _rms_scale(x):
    return x * lax.rsqrt(jnp.mean(x * x, axis=-1, keepdims=True) + EPS)


def _nt_dot(a, b):
    return lax.dot_general(a, b, (((1,), (1,)), ((), ())), preferred_element_type=F32)


def _ffn_kernel(x_ref, g_ref, wg_ref, wu_ref, wd_ref, *rest, emit_normed):
    if emit_normed:
        g2_ref, o_ref, u_ref, xn_ref, acc_ref = rest
    else:
        o_ref, xn_ref, acc_ref = rest
    f = pl.program_id(1)

    @pl.when(f == 0)
    def _():
        xn_ref[...] = (_rms_scale(x_ref[...]) * g_ref[...]).astype(BF16)
        acc_ref[...] = jnp.zeros_like(acc_ref)

    xn = xn_ref[...]
    half = FF_TILE // 2
    acts = []
    for s in range(2):
        cols = slice(s * half, (s + 1) * half)
        gate = jnp.dot(xn, wg_ref[:, cols], preferred_element_type=F32)
        up = jnp.dot(xn, wu_ref[:, cols], preferred_element_type=F32)
        acts.append((gate * jax.nn.sigmoid(gate) * up).astype(BF16))
    acc_ref[...] += (jnp.dot(acts[0], wd_ref[0:half, :], preferred_element_type=F32)
                     + jnp.dot(acts[1], wd_ref[half:, :], preferred_element_type=F32))

    @pl.when(f == pl.num_programs(1) - 1)
    def _():
        out = x_ref[...] + 0.5 * acc_ref[...]
        o_ref[...] = out
        if emit_normed:
            u_ref[...] = (_rms_scale(out) * g2_ref[...]).astype(BF16)


def _ffn(x, layer, gain, w_gate_up, w_down, next_gain=None):
    t, d = x.shape
    tm = min(TOKEN_TILE, t)
    nf = D_FF // FF_TILE
    emit = next_gain is not None
    gain_spec = pl.BlockSpec((None, 1, d), lambda i, f: (layer, 0, 0))
    in_specs = [
        pl.BlockSpec((tm, d), lambda i, f: (i, 0)),
        gain_spec,
        pl.BlockSpec((None, d, FF_TILE), lambda i, f: (layer, 0, f)),
        pl.BlockSpec((None, d, FF_TILE), lambda i, f: (layer, 0, f + nf)),
        pl.BlockSpec((None, FF_TILE, d), lambda i, f: (layer, f, 0)),
    ]
    args = [x, gain, w_gate_up, w_gate_up, w_down]
    out_specs = [pl.BlockSpec((tm, d), lambda i, f: (i, 0))]
    out_shape = [jax.ShapeDtypeStruct((t, d), F32)]
    if emit:
        in_specs.append(gain_spec)
        args.append(next_gain)
        out_specs.append(pl.BlockSpec((tm, d), lambda i, f: (i, 0)))
        out_shape.append(jax.ShapeDtypeStruct((t, d), BF16))
    outs = pl.pallas_call(
        functools.partial(_ffn_kernel, emit_normed=emit),
        grid=(t // tm, nf),
        in_specs=in_specs,
        out_specs=out_specs,
        out_shape=out_shape,
        scratch_shapes=[pltpu.VMEM((tm, d), BF16), pltpu.VMEM((tm, d), F32)],
        compiler_params=_params(("parallel", "arbitrary"), 52),
        name="ffn_norm" if emit else "ffn",
    )(*args)
    return outs if emit else outs[0]


def _proj_kernel(a_ref, w_ref, o_ref, *, squash):
    acc = jnp.dot(a_ref[...], w_ref[...], preferred_element_type=F32)
    o_ref[...] = (jax.nn.sigmoid(acc) if squash else acc).astype(o_ref.dtype)


def _in_proj(u, layer, w_in, first_col, width, squash):
    t, d = u.shape
    tm = min(PROJ_ROWS, t)
    col0 = first_col // PROJ_COLS
    return pl.pallas_call(
        functools.partial(_proj_kernel, squash=squash),
        grid=(t // tm, width // PROJ_COLS),
        in_specs=[pl.BlockSpec((tm, d), lambda i, j: (i, 0)),
                  pl.BlockSpec((None, d, PROJ_COLS), lambda i, j: (layer, 0, col0 + j))],
        out_specs=pl.BlockSpec((tm, PROJ_COLS), lambda i, j: (i, j)),
        out_shape=jax.ShapeDtypeStruct((t, width), BF16 if squash else F32),
        compiler_params=_params(("parallel", "arbitrary"), 40),
        name="gate_proj" if squash else "in_proj",
    )(u, w_in)


def _lower_bound_kernel(logit_ref, lb_ref):
    x = logit_ref[...]
    e = jnp.exp(x - jnp.max(x, axis=0, keepdims=True))
    p = e / jnp.sum(e, axis=0, keepdims=True)
    run = jnp.zeros_like(p[0:1])
    for layer in range(DEPTH):
        run = run + p[layer:layer + 1]
        lb_ref[layer:layer + 1, :] = jnp.clip(run - p[0:1], 0.0, 1.0 - 1e-6)


def _hg_lower_bound(logits):
    flat = logits.astype(F32).reshape(DEPTH, 2 * HG_WIDTH)
    lb = pl.pallas_call(
        _lower_bound_kernel,
        out_shape=jax.ShapeDtypeStruct(flat.shape, F32),
        name="hg_lower_bound",
    )(flat)
    return lb.reshape(DEPTH, 2, HG_WIDTH)


def _s5_prep_kernel(lre_ref, lim_ref, ldt_ref, bre_ref, bim_ref, a_ref, bbar_ref):
    lre = lre_ref[0]
    lim = lim_ref[0]
    dt = jnp.exp(ldt_ref[0])
    mag = jnp.exp(lre * dt)
    a_re = mag * jnp.cos(lim * dt)
    a_im = mag * jnp.sin(lim * dt)
    den = lre * lre + lim * lim
    num_re = a_re - 1.0
    coef_re = (num_re * lre + a_im * lim) / den
    coef_im = (a_im * lre - num_re * lim) / den
    a_ref[0, 0:1, :] = a_re
    a_ref[0, 1:2, :] = a_im
    bre = bre_ref[0]
    bim = bim_ref[0]
    bbar_ref[0, :, 0:HALF_STATES] = (coef_re * bre - coef_im * bim).astype(BF16)
    bbar_ref[0, :, HALF_STATES:] = (coef_re * bim + coef_im * bre).astype(BF16)


def _block_diag_in(b):
    n = b.shape[0]
    per = SSM_GROUPS // S5_HALVES
    b = b.reshape(n * S5_HALVES, per, SSM_STATE, SSM_GROUP)
    full = jnp.einsum('xgnp,gh->xgphn', b, jnp.eye(per, dtype=b.dtype))
    return full.reshape(n * S5_HALVES, HALF_WIDTH, HALF_STATES)


def _block_diag_out(c):
    n = c.shape[0]
    per = SSM_GROUPS // S5_HALVES
    c = c.reshape(n * S5_HALVES, per, SSM_GROUP, SSM_STATE)
    full = jnp.einsum('xgpn,gh->xgnhp', c, jnp.eye(per, dtype=c.dtype))
    return full.reshape(n * S5_HALVES, HALF_STATES, HALF_WIDTH)


def _s5_prepare(lam_re, lam_im, log_dt, b_re, b_im, c_re, c_im):
    n = DEPTH * 2
    m = n * S5_HALVES
    flat = lambda a: a.astype(F32).reshape(m, 1, HALF_STATES)
    ldt = jnp.repeat(log_dt.astype(F32).reshape(n, SSM_GROUPS), SSM_STATE, axis=-1)
    bre = _block_diag_in(b_re.astype(F32).reshape(n, SSM_GROUPS, SSM_STATE, SSM_GROUP))
    bim = _block_diag_in(b_im.astype(F32).reshape(n, SSM_GROUPS, SSM_STATE, SSM_GROUP))
    vec = pl.BlockSpec((1, 1, HALF_STATES), lambda i: (i, 0, 0))
    mat = pl.BlockSpec((1, HALF_WIDTH, HALF_STATES), lambda i: (i, 0, 0))
    a, bbar = pl.pallas_call(
        _s5_prep_kernel,
        grid=(m,),
        in_specs=[vec, vec, vec, mat, mat],
        out_specs=[pl.BlockSpec((1, 2, HALF_STATES), lambda i: (i, 0, 0)),
                   pl.BlockSpec((1, HALF_WIDTH, 2 * HALF_STATES), lambda i: (i, 0, 0))],
        out_shape=[jax.ShapeDtypeStruct((m, 2, HALF_STATES), F32),
                   jax.ShapeDtypeStruct((m, HALF_WIDTH, 2 * HALF_STATES), BF16)],
        compiler_params=_params(("parallel",), 32),
        name="s5_prep",
    )(flat(lam_re), flat(lam_im), flat(ldt), bre, bim)
    a = a.reshape(DEPTH, 2, S5_HALVES, 2, HALF_STATES).transpose(0, 1, 3, 2, 4)
    a = a.reshape(DEPTH, 2, 2, SLABS, LANES)
    bbar = bbar.reshape(DEPTH, 2, S5_HALVES, HALF_WIDTH, 2 * HALF_STATES)
    cre = _block_diag_out(c_re.astype(F32).reshape(n, SSM_GROUPS, SSM_GROUP, SSM_STATE))
    cim = _block_diag_out(c_im.astype(F32).reshape(n, SSM_GROUPS, SSM_GROUP, SSM_STATE))
    cmat = jnp.stack([cre.reshape(DEPTH, 2, S5_HALVES, HALF_STATES, HALF_WIDTH),
                      cim.reshape(DEPTH, 2, S5_HALVES, HALF_STATES, HALF_WIDTH)], axis=2)
    return a, bbar, cmat.astype(BF16)


def _s5_kernel(u_ref, a_ref, bbar_ref, c_ref, d_ref, wglu_ref, o_ref, y_ref, sf_ref, sb_ref):
    seq_len = u_ref.shape[0]
    q = min(S5_CHUNK, seq_len)
    n_chunks = seq_len // q
    im0 = SLABS * S5_PITCH
    y_ref[...] = d_ref[...] * u_ref[...]
    a_f_re, a_f_im = a_ref[0, 0], a_ref[0, 1]
    a_b_re, a_b_im = a_ref[1, 0], a_ref[1, 1]

    def slab_rows(j):
        return slice(j * S5_PITCH, j * S5_PITCH + q)

    def project_in(rows, direction, slab_ref):
        for half in range(S5_HALVES):
            u_half = u_ref[rows, half * HALF_WIDTH:(half + 1) * HALF_WIDTH].astype(BF16)
            bu = jnp.dot(u_half, bbar_ref[direction, half], preferred_element_type=F32)
            for part in range(2):
                for j in range(HALF_SLABS):
                    col = part * HALF_STATES + j * LANES
                    slab_ref[slab_rows(part * SLABS + half * HALF_SLABS + j), :] = bu[:, col:col + LANES]

    def project_out(direction, slab_ref):
        halves = []
        for half in range(S5_HALVES):
            def natural(part):
                first = part * SLABS + half * HALF_SLABS
                return jnp.concatenate([slab_ref[slab_rows(first + j), :] for j in range(HALF_SLABS)],
                                       axis=1).astype(BF16)
            halves.append(jnp.dot(natural(0), c_ref[direction, 0, half], preferred_element_type=F32)
                          - jnp.dot(natural(1), c_ref[direction, 1, half], preferred_element_type=F32))
        return jnp.concatenate(halves, axis=1)

    def advance(slab_ref, t, a_re, a_im, x_re, x_im):
        re_rows = pl.ds(t, SLABS, stride=S5_PITCH)
        im_rows = pl.ds(im0 + t, SLABS, stride=S5_PITCH)
        n_re = a_re * x_re - a_im * x_im + slab_ref[re_rows, :]
        n_im = a_re * x_im + a_im * x_re + slab_ref[im_rows, :]
        slab_ref[re_rows, :] = n_re
        slab_ref[im_rows, :] = n_im
        return n_re, n_im

    def chunk(c, carry):
        rows_f = pl.ds(pl.multiple_of(c * q, q), q)
        rows_b = pl.ds(pl.multiple_of((n_chunks - 1 - c) * q, q), q)
        project_in(rows_f, 0, sf_ref)
        project_in(rows_b, 1, sb_ref)

        def step(i, st):
            f_re, f_im = advance(sf_ref, i, a_f_re, a_f_im, st[0], st[1])
            b_re, b_im = advance(sb_ref, q - 1 - i, a_b_re, a_b_im, st[2], st[3])
            return f_re, f_im, b_re, b_im

        carry = lax.fori_loop(0, q, step, carry, unroll=4)
        y_ref[rows_f, :] += project_out(0, sf_ref)
        y_ref[rows_b, :] += project_out(1, sb_ref)
        return carry

    zero = jnp.zeros((SLABS, LANES), F32)
    lax.fori_loop(0, n_chunks, chunk, (zero, zero, zero, zero))
    z = jax.nn.gelu(y_ref[...])
    glu = jnp.dot(z.astype(BF16), wglu_ref[...], preferred_element_type=F32)
    o_ref[...] = (z * jax.nn.sigmoid(glu)).astype(o_ref.dtype)


def _s5(proj, seq_len, layer, a, bbar, cmat, d_skip, w_glu):
    t = proj.shape[0]

    def whole(shape):
        return _resident((None,) + shape, lambda s: (layer,) + (0,) * len(shape))

    return pl.pallas_call(
        _s5_kernel,
        grid=(t // seq_len,),
        in_specs=[
            pl.BlockSpec((seq_len, SSM_WIDTH), lambda s: (s, OFF_SSM // SSM_WIDTH)),
            whole((2, 2, SLABS, LANES)),
            whole((2, S5_HALVES, HALF_WIDTH, 2 * HALF_STATES)),
            whole((2, 2, S5_HALVES, HALF_STATES, HALF_WIDTH)),
            whole((1, SSM_WIDTH)),
            whole((SSM_WIDTH, SSM_WIDTH)),
        ],
        out_specs=pl.BlockSpec((seq_len, SSM_WIDTH), lambda s: (s, 0)),
        out_shape=jax.ShapeDtypeStruct((t, SSM_WIDTH), BF16),
        scratch_shapes=[pltpu.VMEM((seq_len, SSM_WIDTH), F32),
                        pltpu.VMEM((2 * SLABS * S5_PITCH, LANES), F32),
                        pltpu.VMEM((2 * SLABS * S5_PITCH, LANES), F32)],
        compiler_params=_params(("parallel",), 48),
        name="s5",
    )(proj, a, bbar, cmat, d_skip, w_glu)


def _hg_gates(z, lb):
    e = jnp.exp(-jnp.abs(z))
    r = 1.0 / (1.0 + e)
    er = e * r
    pos = z >= 0.0
    log_f = jnp.log(jnp.maximum(lb, LB_FLOOR) + (1.0 - lb) * jnp.where(pos, r, er))
    return log_f, (1.0 - lb) * jnp.where(pos, er, r)


def _pair_diag(x):
    lane = lax.broadcasted_iota(jnp.int32, x.shape, 1)
    zero = jnp.zeros_like(x)
    return jnp.concatenate([jnp.where(lane < HG_HEAD_DIM, x, zero),
                            jnp.where(lane >= HG_HEAD_DIM, x, zero)], axis=0)


def _hg_reach(c, reverse):
    row = lax.broadcasted_iota(jnp.int32, (c, c), 0)
    col = lax.broadcasted_iota(jnp.int32, (c, c), 1)
    return (col >= row) if reverse else (col <= row)


def _hg_prepare(q_raw, z, vc, lb, reverse):
    c = q_raw.shape[0]
    qc = q_raw * jax.nn.sigmoid(q_raw)
    lfc, kc = _hg_gates(z, lb)
    ones = jnp.where(_hg_reach(c, reverse), 1.0, 0.0).astype(BF16)
    part0 = lfc.astype(BF16)
    rest = lfc - part0.astype(F32)
    part1 = rest.astype(BF16)
    part2 = (rest - part1.astype(F32)).astype(BF16)
    sums = jnp.dot(ones, jnp.concatenate([part0, part1, part2], axis=1), preferred_element_type=F32)
    w = lfc.shape[1]
    cum = sums[:, 0:w] + sums[:, w:2 * w] + sums[:, 2 * w:3 * w]
    total = cum[0:1] if reverse else cum[c - 1:c]
    q_state = (qc * jnp.exp(cum)).astype(BF16)
    k_state = (kc * jnp.exp(total - cum)).astype(BF16)
    v_t = jnp.concatenate([vc[:, 0:HG_HEAD_DIM].T, vc[:, HG_HEAD_DIM:].T], axis=1).astype(BF16)
    return qc, kc, cum, cum - lfc, q_state, k_state, v_t, vc.astype(BF16), jnp.exp(total)


def _hg_state_step(prep, state_t):
    _, _, _, _, q_state, k_state, v_t, _, decay = prep
    out = _nt_dot(q_state, _pair_diag(state_t.astype(BF16)))
    return out, state_t * decay + jnp.dot(v_t, _pair_diag(k_state), preferred_element_type=F32)


def _hg_block_operands(d, step):
    c = d["qc"].shape[0]
    n_blk = c // HG_SUB
    reverse = d["reverse"]
    blk = n_blk - 1 - step if reverse else step
    lo, hi = blk * HG_SUB, (blk + 1) * HG_SUB
    cum = d["cum"]
    ref_row = d["cum_before"][hi - 1:hi] if reverse else d["cum_before"][lo:lo + 1]
    fresh = d["kc"][lo:hi] * jnp.exp(ref_row - cum[lo:hi])
    if d["keys"] is None:
        keys = fresh
    else:
        moved = d["keys"] * jnp.exp(ref_row - d["prev_ref"])
        keys = jnp.concatenate([fresh, moved] if reverse else [moved, fresh], axis=0)
    d["keys"], d["prev_ref"] = keys, ref_row
    padded = keys
    if keys.shape[0] < c:
        blank = jnp.zeros((c - keys.shape[0], keys.shape[1]), F32)
        padded = jnp.concatenate([blank, keys] if reverse else [keys, blank], axis=0)
    q_blk = (d["qc"][lo:hi] * jnp.exp(cum[lo:hi] - ref_row)).astype(BF16)
    return blk, _pair_diag(q_blk), padded.astype(BF16)


def _hg_scores(prep_f, prep_b):
    c = prep_f[0].shape[0]
    n_blk = c // HG_SUB
    dirs = [dict(qc=p[0], kc=p[1], cum=p[2], cum_before=p[3], reverse=rev, keys=None, prev_ref=None,
                 rows=[[None] * n_blk for _ in range(HG_HEADS_PER_STEP)])
            for p, rev in ((prep_f, False), (prep_b, True))]
    for step in range(n_blk):
        blk_f, q_f, k_f = _hg_block_operands(dirs[0], step)
        blk_b, q_b, k_b = _hg_block_operands(dirs[1], step)
        res = _nt_dot(jnp.concatenate([q_f, q_b], axis=0), jnp.concatenate([k_f, k_b], axis=0))
        for head in range(HG_HEADS_PER_STEP):
            dirs[0]["rows"][head][blk_f] = res[head * HG_SUB:(head + 1) * HG_SUB, 0:c]
            dirs[1]["rows"][head][blk_b] = res[(2 + head) * HG_SUB:(3 + head) * HG_SUB, c:2 * c]
    return [jnp.concatenate(
        [jnp.where(_hg_reach(c, d["reverse"]), jnp.concatenate(rows, axis=0), 0.0).astype(BF16)
         for rows in d["rows"]], axis=1) for d in dirs]


def _hgrn_kernel(q_ref, ff_ref, fb_ref, i_ref, g_ref, lb_ref, gain_ref, o_ref, acc_ref):
    seq_len = q_ref.shape[0]
    c = HG_CHUNK
    n_chunks = seq_len // c
    width = HG_HEADS_PER_STEP * HG_HEAD_DIM
    acc_ref[...] = jnp.zeros_like(acc_ref)

    def rows_of(ci):
        return pl.ds(pl.multiple_of(ci * c, c), c)

    def prepare(ci):
        rows_f, rows_b = rows_of(ci), rows_of(n_chunks - 1 - ci)
        return (_hg_prepare(q_ref[rows_f, :], ff_ref[rows_f, :], i_ref[rows_f, :], lb_ref[0:1, :], False),
                _hg_prepare(q_ref[rows_b, :], fb_ref[rows_b, :], i_ref[rows_b, :], lb_ref[1:2, :], True))

    def finish(ci, pending):
        for rows, (out, scores, vb) in zip((rows_of(ci), rows_of(n_chunks - 1 - ci)), pending):
            acc_ref[rows, :] += out + jnp.dot(scores, _pair_diag(vb), preferred_element_type=F32)

    def chunk(ci, carry):
        prep_f, prep_b, state_f, state_b, pending = carry
        finish(jnp.maximum(ci - 1, 0), pending)
        out_f, state_f = _hg_state_step(prep_f, state_f)
        out_b, state_b = _hg_state_step(prep_b, state_b)
        scores_f, scores_b = _hg_scores(prep_f, prep_b)
        pending = ((out_f, scores_f, prep_f[7]), (out_b, scores_b, prep_b[7]))
        next_f, next_b = prepare(jnp.minimum(ci + 1, n_chunks - 1))
        return next_f, next_b, state_f, state_b, pending

    first_f, first_b = prepare(0)
    state0 = jnp.zeros((HG_HEAD_DIM, width), F32)
    idle = (jnp.zeros((c, width), F32), jnp.zeros((c, width), BF16), jnp.zeros((c, width), BF16))
    carry = lax.fori_loop(0, n_chunks, chunk, (first_f, first_b, state0, state0, (idle, idle)))
    finish(n_chunks - 1, carry[4])
    for head in range(HG_HEADS_PER_STEP):
        lanes = slice(head * HG_HEAD_DIM, (head + 1) * HG_HEAD_DIM)
        g_raw = g_ref[:, lanes]
        normed = _rms_scale(acc_ref[:, lanes]) * gain_ref[...]
        o_ref[:, lanes] = (normed * (g_raw * jax.nn.sigmoid(g_raw))).astype(o_ref.dtype)


def _hgrn(proj, seq_len, layer, lower_bound, out_gain):
    t = proj.shape[0]
    width = HG_HEADS_PER_STEP * HG_HEAD_DIM
    col = lambda off: pl.BlockSpec((seq_len, width), lambda s, h, off=off: (s, off // width + h))
    return pl.pallas_call(
        _hgrn_kernel,
        grid=(t // seq_len, HG_HEADS // HG_HEADS_PER_STEP),
        in_specs=[col(OFF_HG_Q), col(OFF_HG_FF), col(OFF_HG_FB), col(OFF_HG_I), col(OFF_HG_G),
                  pl.BlockSpec((None, 2, width), lambda s, h: (layer, 0, h)),
                  pl.BlockSpec((None, 1, HG_HEAD_DIM), lambda s, h: (layer, 0, 0))],
        out_specs=pl.BlockSpec((seq_len, width), lambda s, h: (s, h)),
        out_shape=jax.ShapeDtypeStruct((t, HG_WIDTH), BF16),
        scratch_shapes=[pltpu.VMEM((seq_len, width), F32)],
        compiler_params=_params(("parallel", "parallel"), 40),
        name="hgrn2",
    )(proj, proj, proj, proj, proj, lower_bound, out_gain)


def _rope_tables(seq_len):
    pos = jnp.arange(seq_len, dtype=jnp.int32)
    row = (pos // GRID_W).astype(F32)
    colp = (pos % GRID_W).astype(F32)
    inv_freq = ROPE_BASE ** (-jnp.arange(0, ROPE_AXIS_DIM, 2, dtype=F32) / ROPE_AXIS_DIM)
    ang_r = row[:, None] * inv_freq[None, :]
    ang_c = colp[:, None] * inv_freq[None, :]
    cos = jnp.concatenate([jnp.cos(ang_r)] * 2 + [jnp.cos(ang_c)] * 2, axis=-1)
    sin = jnp.concatenate([-jnp.sin(ang_r), jnp.sin(ang_r), -jnp.sin(ang_c), jnp.sin(ang_c)], axis=-1)
    return cos, sin


def _rope(x, cos, sin):
    half = ROPE_AXIS_DIM // 2
    lane = lax.broadcasted_iota(jnp.int32, x.shape, 1)
    partner = jnp.where(lane % ROPE_AXIS_DIM < half,
                        pltpu.roll(x, HEAD_DIM - half, axis=1), pltpu.roll(x, half, axis=1))
    return x * cos + partner * sin


def _attn_kernel(q_ref, k_ref, v_ref, cosq_ref, sinq_ref, cosk_ref, sink_ref, qg_ref, kg_ref,
                 o_ref, ks_ref, vs_ref):
    @pl.when(pl.program_id(2) == 0)
    def _():
        kn = _rms_scale(k_ref[...]) * kg_ref[...]
        ks_ref[...] = _rope(kn, cosk_ref[...], sink_ref[...]).astype(BF16)
        vs_ref[...] = v_ref[...].astype(BF16)

    scale = HEAD_DIM ** -0.5 * math.log2(math.e)
    keys = ks_ref[...]
    vals = vs_ref[...]
    def logits(g):
        qn = _rms_scale(q_ref[:, g * HEAD_DIM:(g + 1) * HEAD_DIM]) * qg_ref[...]
        return _nt_dot((_rope(qn, cosq_ref[...], sinq_ref[...]) * scale).astype(BF16), keys)

    s_next = logits(0)
    for g in range(ATT_GROUP):
        s = s_next
        if g + 1 < ATT_GROUP:
            s_next = logits(g + 1)
        p = jnp.exp2(s - jnp.max(s, axis=-1, keepdims=True))
        denom = jnp.sum(p, axis=-1, keepdims=True)
        pv = jnp.dot(p.astype(BF16), vals, preferred_element_type=F32)
        o_ref[:, g * HEAD_DIM:(g + 1) * HEAD_DIM] = (pv / denom).astype(o_ref.dtype)


def _attention(proj, seq_len, layer, q_gain, k_gain, cos, sin):
    t = proj.shape[0]
    tq = min(ATT_Q_TILE, seq_len)
    nq = seq_len // tq
    qw = ATT_GROUP * HEAD_DIM
    kv = lambda off: pl.BlockSpec((seq_len, HEAD_DIM), lambda s, h, i, off=off: (s, off // HEAD_DIM + h))
    gain = pl.BlockSpec((None, 1, HEAD_DIM), lambda s, h, i: (layer, 0, 0))
    table_q = pl.BlockSpec((tq, HEAD_DIM), lambda s, h, i: (i, 0))
    table_k = pl.BlockSpec((seq_len, HEAD_DIM), lambda s, h, i: (0, 0))
    return pl.pallas_call(
        _attn_kernel,
        grid=(t // seq_len, ATT_KV_HEADS, nq),
        in_specs=[pl.BlockSpec((tq, qw), lambda s, h, i: (s * nq + i, OFF_ATT_Q // qw + h)),
                  kv(OFF_ATT_K), kv(OFF_ATT_V), table_q, table_q, table_k, table_k, gain, gain],
        out_specs=pl.BlockSpec((tq, qw), lambda s, h, i: (s * nq + i, h)),
        out_shape=jax.ShapeDtypeStruct((t, ATT_WIDTH), BF16),
        scratch_shapes=[pltpu.VMEM((seq_len, HEAD_DIM), BF16), pltpu.VMEM((seq_len, HEAD_DIM), BF16)],
        compiler_params=_params(("parallel", "parallel", "arbitrary"), 40),
        name="attention",
    )(proj, proj, proj, cos, sin, cos, sin, q_gain, k_gain)


def _merge_kernel(h_ref, ys_ref, yh_ref, ya_ref, g_ref, wb_ref, wo_ref, o_ref):
    acc = h_ref[...]
    ys, yh, ya = ys_ref[...], yh_ref[...], ya_ref[...]
    hg0, att0 = SSM_WIDTH, SSM_WIDTH + HG_WIDTH
    for c in range(D_MODEL // MERGE_TILE):
        cols = slice(c * MERGE_TILE, (c + 1) * MERGE_TILE)
        gate = lambda b: g_ref[:, b * D_MODEL + c * MERGE_TILE:b * D_MODEL + (c + 1) * MERGE_TILE]
        merged = (gate(0) * jnp.dot(ys, wb_ref[0:hg0, cols], preferred_element_type=F32)
                  + gate(1) * jnp.dot(yh, wb_ref[hg0:att0, cols], preferred_element_type=F32)
                  + gate(2) * jnp.dot(ya, wb_ref[att0:, cols], preferred_element_type=F32))
        acc = acc + jnp.dot(merged.astype(BF16), wo_ref[cols, :], preferred_element_type=F32)
    o_ref[...] = acc


def _merge(h, gates, y_ssm, y_hg, y_att, layer, w_branch, w_out):
    t, d = h.shape
    tm = min(TOKEN_TILE, t)
    rows = lambda width: pl.BlockSpec((tm, width), lambda i: (i, 0))
    weight = _resident((None, d, d), lambda i: (layer, 0, 0))
    return pl.pallas_call(
        _merge_kernel,
        grid=(t // tm,),
        in_specs=[rows(d), rows(SSM_WIDTH), rows(HG_WIDTH), rows(ATT_WIDTH), rows(GATE_WIDTH),
                  weight, weight],
        out_specs=rows(d),
        out_shape=jax.ShapeDtypeStruct((t, d), F32),
        compiler_params=_params(("parallel",), 60),
        name="merge",
    )(h, y_ssm, y_hg, y_att, gates, w_branch, w_out)


def _trunk(x, p):
    n_seq, seq_len, d = x.shape
    cos, sin = _rope_tables(seq_len)
    h = x.reshape(n_seq * seq_len, d)
    for layer in range(DEPTH):
        h, u = _ffn(h, layer, p["ffn1_norm"], p["ffn1_w_gate_up"], p["ffn1_w_down"], p["mix_norm"])
        proj = _in_proj(u, layer, p["w_in"], 0, MAIN_WIDTH, False)
        gates = _in_proj(u, layer, p["w_in"], MAIN_WIDTH, GATE_WIDTH, True)
        y_ssm = _s5(proj, seq_len, layer, p["s5_a"], p["s5_bbar"], p["s5_c"], p["ssm_d"], p["ssm_w_glu"])
        y_hg = _hgrn(proj, seq_len, layer, p["hg_lb"], p["hg_out_norm"])
        y_att = _attention(proj, seq_len, layer, p["att_q_norm"], p["att_k_norm"], cos, sin)
        h = _merge(h, gates, y_ssm, y_hg, y_att, layer, p["w_branch"], p["w_out"])
        h = _ffn(h, layer, p["ffn2_norm"], p["ffn2_w_gate_up"], p["ffn2_w_down"])
    return h.reshape(n_seq, seq_len, d)


def kernel(x_prompt, x_sample, ffn1_norm, ffn1_w_gate_up, ffn1_w_down, mix_norm, w_in, ssm_lambda_re, ssm_lambda_im, ssm_log_dt, ssm_b_re, ssm_b_im, ssm_c_re, ssm_c_im, ssm_d, ssm_w_glu, hg_lb_logits, hg_out_norm, att_q_norm, att_k_norm, w_branch, w_out, ffn2_norm, ffn2_w_gate_up, ffn2_w_down):
    s5_a, s5_bbar, s5_c = _s5_prepare(ssm_lambda_re, ssm_lambda_im, ssm_log_dt,
                                      ssm_b_re, ssm_b_im, ssm_c_re, ssm_c_im)
    row = lambda a: a.astype(F32).reshape(DEPTH, 1, a.shape[-1])
    bf16 = lambda a: a.astype(BF16)
    p = dict(
        ffn1_norm=row(ffn1_norm), ffn1_w_gate_up=bf16(ffn1_w_gate_up), ffn1_w_down=bf16(ffn1_w_down),
        mix_norm=row(mix_norm), w_in=bf16(w_in),
        s5_a=s5_a, s5_bbar=s5_bbar, s5_c=s5_c, ssm_d=row(ssm_d), ssm_w_glu=bf16(ssm_w_glu),
        hg_lb=_hg_lower_bound(hg_lb_logits), hg_out_norm=row(hg_out_norm),
        att_q_norm=row(att_q_norm), att_k_norm=row(att_k_norm),
        w_branch=bf16(w_branch), w_out=bf16(w_out),
        ffn2_norm=row(ffn2_norm), ffn2_w_gate_up=bf16(ffn2_w_gate_up), ffn2_w_down=bf16(ffn2_w_down),
    )
    return _trunk(x_prompt, p), _trunk(x_sample, p)
```

```python
import functools
import math

import jax
import jax.numpy as jnp
from jax import lax
from jax.experimental import pallas as pl
from jax.experimental.pallas import tpu as pltpu

F32 = jnp.float32
BF16 = jnp.bfloat16

D_MODEL = 2048
DEPTH = 4
GRID_W = 64
EPS = 1e-6
LB_FLOOR = 1e-30
SSM_WIDTH = 512
SSM_GROUP = 16
SSM_GROUPS = 32
SSM_STATE = 64
SSM_STATES = SSM_GROUPS * SSM_STATE
HG_HEAD_DIM = 128
HG_WIDTH = 512
HG_HEADS = 4
HEAD_DIM = 128
ATT_Q_HEADS = 8
ATT_KV_HEADS = 2
ATT_GROUP = ATT_Q_HEADS // ATT_KV_HEADS
ATT_WIDTH = 1024
KV_WIDTH = 256
ROPE_BASE = 10000.0
ROPE_AXIS_DIM = HEAD_DIM // 2
N_BRANCH = 3
D_FF = 5632
GATE_WIDTH = N_BRANCH * D_MODEL
MAIN_WIDTH = SSM_WIDTH + 5 * HG_WIDTH + ATT_WIDTH + 2 * KV_WIDTH

OFF_SSM = 0
OFF_HG_Q = 512
OFF_HG_FF = 1024
OFF_HG_FB = 1536
OFF_HG_I = 2048
OFF_HG_G = 2560
OFF_ATT_Q = 3072
OFF_ATT_K = 4096
OFF_ATT_V = 4352

LANES = 128
S5_BITS = 4
S5_BLOCK = 1 << S5_BITS
S5_ROW = S5_BLOCK * SSM_GROUP
S5_FINISH_ROWS = 1024

TOKEN_TILE = 512
FF_TILE = 512
PROJ_ROWS = 1024
PROJ_COLS = 1536
MERGE_TILE = 512
ATT_Q_TILE = 256
HG_CHUNK = 128
HG_SUB = 16
HG_HEADS_PER_STEP = 2

MIB = 1024 * 1024


def _params(semantics, vmem_mib):
    return pltpu.CompilerParams(dimension_semantics=semantics, vmem_limit_bytes=vmem_mib * MIB)


def _resident(shape, index_map):
    return pl.BlockSpec(shape, index_map, pipeline_mode=pl.Buffered(1))


def _rms_scale(x):
    return x * lax.rsqrt(jnp.mean(x * x, axis=-1, keepdims=True) + EPS)


def _nt_dot(a, b):
    return lax.dot_general(a, b, (((1,), (1,)), ((), ())), preferred_element_type=F32)


def _ffn_kernel(x_ref, g_ref, wg_ref, wu_ref, wd_ref, *rest, emit_normed):
    if emit_normed:
        g2_ref, o_ref, u_ref, xn_ref, acc_ref = rest
    else:
        o_ref, xn_ref, acc_ref = rest
    f = pl.program_id(1)

    @pl.when(f == 0)
    def _():
        xn_ref[...] = (_rms_scale(x_ref[...]) * g_ref[...]).astype(BF16)
        acc_ref[...] = jnp.zeros_like(acc_ref)

    xn = xn_ref[...]
    half = FF_TILE // 2
    acts = []
    for s in range(2):
        cols = slice(s * half, (s + 1) * half)
        gate = jnp.dot(xn, wg_ref[:, cols], preferred_element_type=F32)
        up = jnp.dot(xn, wu_ref[:, cols], preferred_element_type=F32)
        acts.append((gate * jax.nn.sigmoid(gate) * up).astype(BF16))
    acc_ref[...] += (jnp.dot(acts[0], wd_ref[0:half, :], preferred_element_type=F32)
                     + jnp.dot(acts[1], wd_ref[half:, :], preferred_element_type=F32))

    @pl.when(f == pl.num_programs(1) - 1)
    def _():
        out = x_ref[...] + 0.5 * acc_ref[...]
        o_ref[...] = out
        if emit_normed:
            u_ref[...] = (_rms_scale(out) * g2_ref[...]).astype(BF16)


def _ffn(x, layer, gain, w_gate_up, w_down, next_gain=None):
    t, d = x.shape
    tm = min(TOKEN_TILE, t)
    nf = D_FF // FF_TILE
    emit = next_gain is not None
    gain_spec = pl.BlockSpec((None, 1, d), lambda i, f: (layer, 0, 0))
    in_specs = [
        pl.BlockSpec((tm, d), lambda i, f: (i, 0)),
        gain_spec,
        pl.BlockSpec((None, d, FF_TILE), lambda i, f: (layer, 0, f)),
        pl.BlockSpec((None, d, FF_TILE), lambda i, f: (layer, 0, f + nf)),
        pl.BlockSpec((None, FF_TILE, d), lambda i, f: (layer, f, 0)),
    ]
    args = [x, gain, w_gate_up, w_gate_up, w_down]
    out_specs = [pl.BlockSpec((tm, d), lambda i, f: (i, 0))]
    out_shape = [jax.ShapeDtypeStruct((t, d), F32)]
    if emit:
        in_specs.append(gain_spec)
        args.append(next_gain)
        out_specs.append(pl.BlockSpec((tm, d), lambda i, f: (i, 0)))
        out_shape.append(jax.ShapeDtypeStruct((t, d), BF16))
    outs = pl.pallas_call(
        functools.partial(_ffn_kernel, emit_normed=emit),
        grid=(t // tm, nf),
        in_specs=in_specs,
        out_specs=out_specs,
        out_shape=out_shape,
        scratch_shapes=[pltpu.VMEM((tm, d), BF16), pltpu.VMEM((tm, d), F32)],
        compiler_params=_params(("parallel", "arbitrary"), 52),
        name="ffn_norm" if emit else "ffn",
    )(*args)
    return outs if emit else outs[0]


def _proj_kernel(a_ref, w_ref, o_ref, *, squash):
    acc = jnp.dot(a_ref[...], w_ref[...], preferred_element_type=F32)
    o_ref[...] = (jax.nn.sigmoid(acc) if squash else acc).astype(o_ref.dtype)


def _in_proj(u, layer, w_in, first_col, width, squash):
    t, d = u.shape
    tm = min(PROJ_ROWS, t)
    col0 = first_col // PROJ_COLS
    return pl.pallas_call(
        functools.partial(_proj_kernel, squash=squash),
        grid=(t // tm, width // PROJ_COLS),
        in_specs=[pl.BlockSpec((tm, d), lambda i, j: (i, 0)),
                  pl.BlockSpec((None, d, PROJ_COLS), lambda i, j: (layer, 0, col0 + j))],
        out_specs=pl.BlockSpec((tm, PROJ_COLS), lambda i, j: (i, j)),
        out_shape=jax.ShapeDtypeStruct((t, width), BF16 if squash else F32),
        compiler_params=_params(("parallel", "arbitrary"), 40),
        name="gate_proj" if squash else "in_proj",
    )(u, w_in)


def _lower_bound_kernel(logit_ref, lb_ref):
    x = logit_ref[...]
    e = jnp.exp(x - jnp.max(x, axis=0, keepdims=True))
    p = e / jnp.sum(e, axis=0, keepdims=True)
    run = jnp.zeros_like(p[0:1])
    for layer in range(DEPTH):
        run = run + p[layer:layer + 1]
        lb_ref[layer:layer + 1, :] = jnp.clip(run - p[0:1], 0.0, 1.0 - 1e-6)


def _hg_lower_bound(logits):
    flat = logits.astype(F32).reshape(DEPTH, 2 * HG_WIDTH)
    lb = pl.pallas_call(
        _lower_bound_kernel,
        out_shape=jax.ShapeDtypeStruct(flat.shape, F32),
        name="hg_lower_bound",
    )(flat)
    return lb.reshape(DEPTH, 2, HG_WIDTH)


def _cmul(ar, ai, br, bi):
    return ar * br - ai * bi, ar * bi + ai * br


def _s5_prep_kernel(lre_ref, lim_ref, ldt_ref, lane_ref, bre_ref, bim_ref, cre_ref, cim_ref,
                    ctre_ref, ctim_ref, g_ref, wy_ref, kk_ref, pw_ref, *, scan_steps):
    lre, lim = lre_ref[0], lim_ref[0]
    dt = jnp.exp(ldt_ref[0])
    mag = jnp.exp(lre * dt)
    ar, ai = mag * jnp.cos(lim * dt), mag * jnp.sin(lim * dt)
    den = lre * lre + lim * lim
    coef_re = ((ar - 1.0) * lre + ai * lim) / den
    coef_im = (ai * lre - (ar - 1.0) * lim) / den
    k_idx = lax.broadcasted_iota(jnp.int32, (1, 1, S5_ROW), 2) // SSM_GROUP
    shape = (SSM_GROUPS, SSM_STATE, S5_ROW)
    pr, pi = jnp.ones(shape, F32), jnp.zeros(shape, F32)
    sr, si = ar, ai
    for bit in range(S5_BITS):
        on = ((k_idx >> bit) & 1) == 1
        pr, pi = _cmul(pr, pi, jnp.where(on, sr, 1.0), jnp.where(on, si, 0.0))
        sr, si = _cmul(sr, si, sr, si)
    bb_re, bb_im = _cmul(coef_re, coef_im, bre_ref[0], bim_ref[0])
    g_re, g_im = _cmul(pr, pi, bb_re, bb_im)
    g_ref[0, 0] = g_re
    g_ref[0, 1] = g_im
    p1r, p1i = _cmul(pr, pi, ar, ai)
    w_re, w_im = _cmul(ctre_ref[0], ctim_ref[0], p1r, p1i)
    wy_ref[0, :, 0:SSM_STATE, :] = w_re
    wy_ref[0, :, SSM_STATE:, :] = -w_im
    for g in range(SSM_GROUPS):
        kk_ref[0, g] = (jnp.dot(cre_ref[0, g], g_re[g], precision=lax.Precision.HIGHEST,
                                preferred_element_type=F32)
                        - jnp.dot(cim_ref[0, g], g_im[g], precision=lax.Precision.HIGHEST,
                                  preferred_element_type=F32))
    l_re, l_im, l_dt = lane_ref[0, 0], lane_ref[0, 1], jnp.exp(lane_ref[0, 2])
    m = jnp.exp(l_re * l_dt)
    qr, qi = m * jnp.cos(l_im * l_dt), m * jnp.sin(l_im * l_dt)
    for _ in range(S5_BITS):
        qr, qi = _cmul(qr, qi, qr, qi)
    sign = jnp.where(lax.broadcasted_iota(jnp.int32, qr.shape, 1) < SSM_STATE, -1.0, 1.0)
    for j in range(scan_steps):
        pw_ref[0, :, 2 * j, :] = qr
        pw_ref[0, :, 2 * j + 1, :] = qi * sign
        qr, qi = _cmul(qr, qi, qr, qi)


def _s5_prepare(lam_re, lam_im, log_dt, b_re, b_im, c_re, c_im, scan_steps):
    n = DEPTH * 2
    g, s, p, r = SSM_GROUPS, SSM_STATE, SSM_GROUP, S5_BLOCK
    col = lambda a: a.astype(F32).reshape(n, g, s, 1)
    ldt = jnp.broadcast_to(log_dt.astype(F32).reshape(n, g, 1), (n, g, s))
    dup = lambda a: jnp.concatenate([a, a], axis=-1)
    lane = jnp.stack([dup(lam_re.astype(F32).reshape(n, g, s)), dup(lam_im.astype(F32).reshape(n, g, s)),
                      dup(ldt)], axis=1)
    tile = lambda a: jnp.tile(a, (1, 1, 1, r))
    bre = tile(b_re.astype(F32).reshape(n, g, s, p))
    bim = tile(b_im.astype(F32).reshape(n, g, s, p))
    cre = c_re.astype(F32).reshape(n, g, p, s)
    cim = c_im.astype(F32).reshape(n, g, p, s)
    ctre = tile(cre.transpose(0, 1, 3, 2))
    ctim = tile(cim.transpose(0, 1, 3, 2))
    b4 = lambda last2: pl.BlockSpec((1, g) + last2, lambda i: (i, 0, 0, 0))
    gg, wy, kk, pw = pl.pallas_call(
        functools.partial(_s5_prep_kernel, scan_steps=scan_steps),
        grid=(n,),
        in_specs=[b4((s, 1)), b4((s, 1)), b4((s, 1)),
                  pl.BlockSpec((1, 3, g, 2 * s), lambda i: (i, 0, 0, 0)),
                  b4((s, S5_ROW)), b4((s, S5_ROW)), b4((p, s)), b4((p, s)), b4((s, S5_ROW)), b4((s, S5_ROW))],
        out_specs=[pl.BlockSpec((1, 2, g, s, S5_ROW), lambda i: (i, 0, 0, 0, 0)),
                   b4((2 * s, S5_ROW)), b4((p, S5_ROW)), b4((2 * scan_steps, 2 * s))],
        out_shape=[jax.ShapeDtypeStruct((n, 2, g, s, S5_ROW), F32),
                   jax.ShapeDtypeStruct((n, g, 2 * s, S5_ROW), F32),
                   jax.ShapeDtypeStruct((n, g, p, S5_ROW), F32),
                   jax.ShapeDtypeStruct((n, g, 2 * scan_steps, 2 * s), F32)],
        compiler_params=_params(("parallel",), 56),
        name="s5_prep",
    )(col(lam_re), col(lam_im), ldt.reshape(n, g, s, 1), lane, bre, bim, cre, cim, ctre, ctim)
    lag = jnp.arange(r)
    shift = (lag[None, :, None] + lag[:, None, None] == lag[None, None, :]).astype(F32)
    toep = jnp.einsum('xgpkq,ksr->xgsqrp', kk.reshape(n, g, p, r, p), shift)
    we = gg.reshape(n, 2, g, s, r, p)[:, :, :, :, ::-1, :].transpose(0, 2, 4, 5, 1, 3)
    we = we.reshape(n, g, r, p, 2 * s)
    wyr = wy.reshape(n, g, 2 * s, r, p)

    def both(a, flips):
        a = a.reshape((DEPTH, 2) + a.shape[1:])
        bwd = a[:, 1]
        for axis in flips:
            bwd = jnp.flip(bwd, axis=axis)
        return a[:, 0], bwd

    t_f, t_b = both(toep, (2, 4))
    we_f, we_b = both(we, (2,))
    wy_f, wy_b = both(wyr, (3,))
    flat = lambda a, rows, cols: a.reshape(DEPTH, g, rows, cols)
    w1 = jnp.concatenate([flat(t_f, S5_ROW, S5_ROW), flat(t_b, S5_ROW, S5_ROW),
                          flat(we_f, S5_ROW, 2 * s), flat(we_b, S5_ROW, 2 * s)], axis=-1).astype(BF16)
    w2 = jnp.concatenate([flat(wy_f, 2 * s, S5_ROW), flat(wy_b, 2 * s, S5_ROW)], axis=-2).astype(BF16)
    return w1, w2, pw.reshape(DEPTH, 2, g, 2 * scan_steps, 2 * s)


def _s5_block_scan(e, pw_ref, direction, upward):
    n_blk = e.shape[0]
    row = lax.broadcasted_iota(jnp.int32, e.shape, 0)

    def shifted(x, by):
        if upward:
            return jnp.where(row < n_blk - by, pltpu.roll(x, n_blk - by, axis=0), 0.0)
        return jnp.where(row >= by, pltpu.roll(x, by, axis=0), 0.0)

    x = e
    step = 0
    while (1 << step) < n_blk:
        moved = shifted(x, 1 << step)
        x = (x + moved * pw_ref[direction, 2 * step:2 * step + 1, :]
             + pltpu.roll(moved, SSM_STATE, axis=1) * pw_ref[direction, 2 * step + 1:2 * step + 2, :])
        step += 1
    return shifted(x, 1)


def _s5_core_kernel(u_ref, w1_ref, w2_ref, pw_ref, y_ref, e_ref, xp_ref, *, blocks_per_seq):
    first = jnp.dot(u_ref[...], w1_ref[...], preferred_element_type=F32)
    y_ref[...] = first[:, 0:S5_ROW] + first[:, S5_ROW:2 * S5_ROW]
    e_ref[...] = first[:, 2 * S5_ROW:]
    n_seq = u_ref.shape[0] // blocks_per_seq

    def one_sequence(s, carry):
        rows = pl.ds(pl.multiple_of(s * blocks_per_seq, blocks_per_seq), blocks_per_seq)
        xp_ref[rows, 0:LANES] = _s5_block_scan(e_ref[rows, 0:LANES], pw_ref, 0, False).astype(BF16)
        xp_ref[rows, LANES:] = _s5_block_scan(e_ref[rows, LANES:], pw_ref, 1, True).astype(BF16)
        return carry

    lax.fori_loop(0, n_seq, one_sequence, 0)
    y_ref[...] += jnp.dot(xp_ref[...], w2_ref[...], preferred_element_type=F32)


def _s5_finish_kernel(y_ref, u_ref, d_ref, wglu_ref, o_ref):
    z = jax.nn.gelu(y_ref[...] + d_ref[...] * u_ref[...])
    glu = jnp.dot(z.astype(BF16), wglu_ref[...], preferred_element_type=F32)
    o_ref[...] = (z * jax.nn.sigmoid(glu)).astype(o_ref.dtype)


def _s5(proj, seq_len, layer, w1, w2, pw, d_skip, w_glu):
    t = proj.shape[0]
    m = t // S5_BLOCK
    blocks_per_seq = seq_len // S5_BLOCK
    u = proj[:, OFF_SSM:OFF_SSM + SSM_WIDTH].astype(BF16).reshape(m, S5_BLOCK, SSM_GROUPS, SSM_GROUP)
    u = u.transpose(2, 0, 1, 3).reshape(SSM_GROUPS, m, S5_ROW)
    weights = lambda shape: pl.BlockSpec((None, None) + shape, lambda g: (layer, g, 0, 0))
    y = pl.pallas_call(
        functools.partial(_s5_core_kernel, blocks_per_seq=blocks_per_seq),
        grid=(SSM_GROUPS,),
        in_specs=[pl.BlockSpec((None, m, S5_ROW), lambda g: (g, 0, 0)),
                  weights((S5_ROW, 3 * S5_ROW)), weights((S5_ROW, S5_ROW)),
                  pl.BlockSpec((None, 2, None, pw.shape[3], LANES), lambda g: (layer, 0, g, 0, 0))],
        out_specs=pl.BlockSpec((None, m, S5_ROW), lambda g: (g, 0, 0)),
        out_shape=jax.ShapeDtypeStruct((SSM_GROUPS, m, S5_ROW), F32),
        scratch_shapes=[pltpu.VMEM((m, S5_ROW), F32), pltpu.VMEM((m, S5_ROW), BF16)],
        compiler_params=_params(("parallel",), 40),
        name="s5_core",
    )(u, w1, w2, pw)
    y = y.reshape(SSM_GROUPS, m, S5_BLOCK, SSM_GROUP).transpose(1, 2, 0, 3).reshape(t, SSM_WIDTH)
    tm = min(S5_FINISH_ROWS, t)
    rows = pl.BlockSpec((tm, SSM_WIDTH), lambda i: (i, 0))
    return pl.pallas_call(
        _s5_finish_kernel,
        grid=(t // tm,),
        in_specs=[rows, pl.BlockSpec((tm, SSM_WIDTH), lambda i: (i, OFF_SSM // SSM_WIDTH)),
                  pl.BlockSpec((None, 1, SSM_WIDTH), lambda i: (layer, 0, 0)),
                  pl.BlockSpec((None, SSM_WIDTH, SSM_WIDTH), lambda i: (layer, 0, 0))],
        out_specs=rows,
        out_shape=jax.ShapeDtypeStruct((t, SSM_WIDTH), BF16),
        compiler_params=_params(("parallel",), 32),
        name="s5_finish",
    )(y, proj, d_skip, w_glu)


def _hg_gates(z, lb):
    e = jnp.exp(-jnp.abs(z))
    r = 1.0 / (1.0 + e)
    er = e * r
    pos = z >= 0.0
    log_f = jnp.log(jnp.maximum(lb, LB_FLOOR) + (1.0 - lb) * jnp.where(pos, r, er))
    return log_f, (1.0 - lb) * jnp.where(pos, er, r)


def _pair_diag(x):
    lane = lax.broadcasted_iota(jnp.int32, x.shape, 1)
    zero = jnp.zeros_like(x)
    return jnp.concatenate([jnp.where(lane < HG_HEAD_DIM, x, zero),
                            jnp.where(lane >= HG_HEAD_DIM, x, zero)], axis=0)


def _hg_reach(c, reverse):
    row = lax.broadcasted_iota(jnp.int32, (c, c), 0)
    col = lax.broadcasted_iota(jnp.int32, (c, c), 1)
    return (col >= row) if reverse else (col <= row)


def _hg_prepare(q_raw, z, vc, lb, reverse):
    c = q_raw.shape[0]
    qc = q_raw * jax.nn.sigmoid(q_raw)
    lfc, kc = _hg_gates(z, lb)
    ones = jnp.where(_hg_reach(c, reverse), 1.0, 0.0).astype(BF16)
    part0 = lfc.astype(BF16)
    rest = lfc - part0.astype(F32)
    part1 = rest.astype(BF16)
    part2 = (rest - part1.astype(F32)).astype(BF16)
    sums = jnp.dot(ones, jnp.concatenate([part0, part1, part2], axis=1), preferred_element_type=F32)
    w = lfc.shape[1]
    cum = sums[:, 0:w] + sums[:, w:2 * w] + sums[:, 2 * w:3 * w]
    total = cum[0:1] if reverse else cum[c - 1:c]
    q_state = (qc * jnp.exp(cum)).astype(BF16)
    k_state = (kc * jnp.exp(total - cum)).astype(BF16)
    v_t = jnp.concatenate([vc[:, 0:HG_HEAD_DIM].T, vc[:, HG_HEAD_DIM:].T], axis=1).astype(BF16)
    return qc, kc, cum, cum - lfc, q_state, k_state, v_t, vc.astype(BF16), jnp.exp(total)


def _hg_state_step(prep, state_t):
    _, _, _, _, q_state, k_state, v_t, _, decay = prep
    out = _nt_dot(q_state, _pair_diag(state_t.astype(BF16)))
    return out, state_t * decay + jnp.dot(v_t, _pair_diag(k_state), preferred_element_type=F32)


def _hg_block_operands(d, step):
    c = d["qc"].shape[0]
    n_blk = c // HG_SUB
    reverse = d["reverse"]
    blk = n_blk - 1 - step if reverse else step
    lo, hi = blk * HG_SUB, (blk + 1) * HG_SUB
    cum = d["cum"]
    ref_row = d["cum_before"][hi - 1:hi] if reverse else d["cum_before"][lo:lo + 1]
    fresh = d["kc"][lo:hi] * jnp.exp(ref_row - cum[lo:hi])
    if d["keys"] is None:
        keys = fresh
    else:
        moved = d["keys"] * jnp.exp(ref_row - d["prev_ref"])
        keys = jnp.concatenate([fresh, moved] if reverse else [moved, fresh], axis=0)
    d["keys"], d["prev_ref"] = keys, ref_row
    padded = keys
    if keys.shape[0] < c:
        blank = jnp.zeros((c - keys.shape[0], keys.shape[1]), F32)
        padded = jnp.concatenate([blank, keys] if reverse else [keys, blank], axis=0)
    q_blk = (d["qc"][lo:hi] * jnp.exp(cum[lo:hi] - ref_row)).astype(BF16)
    return blk, _pair_diag(q_blk), padded.astype(BF16)


def _hg_scores(prep_f, prep_b):
    c = prep_f[0].shape[0]
    n_blk = c // HG_SUB
    dirs = [dict(qc=p[0], kc=p[1], cum=p[2], cum_before=p[3], reverse=rev, keys=None, prev_ref=None,
                 rows=[[None] * n_blk for _ in range(HG_HEADS_PER_STEP)])
            for p, rev in ((prep_f, False), (prep_b, True))]
    for step in range(n_blk):
        blk_f, q_f, k_f = _hg_block_operands(dirs[0], step)
        blk_b, q_b, k_b = _hg_block_operands(dirs[1], step)
        res = _nt_dot(jnp.concatenate([q_f, q_b], axis=0), jnp.concatenate([k_f, k_b], axis=0))
        for head in range(HG_HEADS_PER_STEP):
            dirs[0]["rows"][head][blk_f] = res[head * HG_SUB:(head + 1) * HG_SUB, 0:c]
            dirs[1]["rows"][head][blk_b] = res[(2 + head) * HG_SUB:(3 + head) * HG_SUB, c:2 * c]
    return [jnp.concatenate(
        [jnp.where(_hg_reach(c, d["reverse"]), jnp.concatenate(rows, axis=0), 0.0).astype(BF16)
         for rows in d["rows"]], axis=1) for d in dirs]


def _hgrn_kernel(q_ref, ff_ref, fb_ref, i_ref, g_ref, lb_ref, gain_ref, o_ref, acc_ref):
    seq_len = q_ref.shape[0]
    c = HG_CHUNK
    n_chunks = seq_len // c
    width = HG_HEADS_PER_STEP * HG_HEAD_DIM
    acc_ref[...] = jnp.zeros_like(acc_ref)

    def rows_of(ci):
        return pl.ds(pl.multiple_of(ci * c, c), c)

    def prepare(ci):
        rows_f, rows_b = rows_of(ci), rows_of(n_chunks - 1 - ci)
        return (_hg_prepare(q_ref[rows_f, :], ff_ref[rows_f, :], i_ref[rows_f, :], lb_ref[0:1, :], False),
                _hg_prepare(q_ref[rows_b, :], fb_ref[rows_b, :], i_ref[rows_b, :], lb_ref[1:2, :], True))

    def finish(ci, pending):
        for rows, (out, scores, vb) in zip((rows_of(ci), rows_of(n_chunks - 1 - ci)), pending):
            acc_ref[rows, :] += out + jnp.dot(scores, _pair_diag(vb), preferred_element_type=F32)

    def chunk(ci, carry):
        prep_f, prep_b, state_f, state_b, pending = carry
        finish(jnp.maximum(ci - 1, 0), pending)
        out_f, state_f = _hg_state_step(prep_f, state_f)
        out_b, state_b = _hg_state_step(prep_b, state_b)
        scores_f, scores_b = _hg_scores(prep_f, prep_b)
        pending = ((out_f, scores_f, prep_f[7]), (out_b, scores_b, prep_b[7]))
        next_f, next_b = prepare(jnp.minimum(ci + 1, n_chunks - 1))
        return next_f, next_b, state_f, state_b, pending

    first_f, first_b = prepare(0)
    state0 = jnp.zeros((HG_HEAD_DIM, width), F32)
    idle = (jnp.zeros((c, width), F32), jnp.zeros((c, width), BF16), jnp.zeros((c, width), BF16))
    carry = lax.fori_loop(0, n_chunks, chunk, (first_f, first_b, state0, state0, (idle, idle)))
    finish(n_chunks - 1, carry[4])
    for head in range(HG_HEADS_PER_STEP):
        lanes = slice(head * HG_HEAD_DIM, (head + 1) * HG_HEAD_DIM)
        g_raw = g_ref[:, lanes]
        normed = _rms_scale(acc_ref[:, lanes]) * gain_ref[...]
        o_ref[:, lanes] = (normed * (g_raw * jax.nn.sigmoid(g_raw))).astype(o_ref.dtype)


def _hgrn(proj, seq_len, layer, lower_bound, out_gain):
    t = proj.shape[0]
    width = HG_HEADS_PER_STEP * HG_HEAD_DIM
    col = lambda off: pl.BlockSpec((seq_len, width), lambda s, h, off=off: (s, off // width + h))
    return pl.pallas_call(
        _hgrn_kernel,
        grid=(t // seq_len, HG_HEADS // HG_HEADS_PER_STEP),
        in_specs=[col(OFF_HG_Q), col(OFF_HG_FF), col(OFF_HG_FB), col(OFF_HG_I), col(OFF_HG_G),
                  pl.BlockSpec((None, 2, width), lambda s, h: (layer, 0, h)),
                  pl.BlockSpec((None, 1, HG_HEAD_DIM), lambda s, h: (layer, 0, 0))],
        out_specs=pl.BlockSpec((seq_len, width), lambda s, h: (s, h)),
        out_shape=jax.ShapeDtypeStruct((t, HG_WIDTH), BF16),
        scratch_shapes=[pltpu.VMEM((seq_len, width), F32)],
        compiler_params=_params(("parallel", "parallel"), 40),
        name="hgrn2",
    )(proj, proj, proj, proj, proj, lower_bound, out_gain)


def _rope_tables(seq_len):
    pos = jnp.arange(seq_len, dtype=jnp.int32)
    row = (pos // GRID_W).astype(F32)
    colp = (pos % GRID_W).astype(F32)
    inv_freq = ROPE_BASE ** (-jnp.arange(0, ROPE_AXIS_DIM, 2, dtype=F32) / ROPE_AXIS_DIM)
    ang_r = row[:, None] * inv_freq[None, :]
    ang_c = colp[:, None] * inv_freq[None, :]
    cos = jnp.concatenate([jnp.cos(ang_r)] * 2 + [jnp.cos(ang_c)] * 2, axis=-1)
    sin = jnp.concatenate([-jnp.sin(ang_r), jnp.sin(ang_r), -jnp.sin(ang_c), jnp.sin(ang_c)], axis=-1)
    return cos, sin


def _rope(x, cos, sin):
    half = ROPE_AXIS_DIM // 2
    lane = lax.broadcasted_iota(jnp.int32, x.shape, 1)
    partner = jnp.where(lane % ROPE_AXIS_DIM < half,
                        pltpu.roll(x, HEAD_DIM - half, axis=1), pltpu.roll(x, half, axis=1))
    return x * cos + partner * sin


def _attn_kernel(q_ref, k_ref, v_ref, cosq_ref, sinq_ref, cosk_ref, sink_ref, qg_ref, kg_ref,
                 o_ref, ks_ref, vs_ref):
    @pl.when(pl.program_id(2) == 0)
    def _():
        kn = _rms_scale(k_ref[...]) * kg_ref[...]
        ks_ref[...] = _rope(kn, cosk_ref[...], sink_ref[...]).astype(BF16)
        vs_ref[...] = v_ref[...].astype(BF16)

    scale = HEAD_DIM ** -0.5 * math.log2(math.e)
    keys = ks_ref[...]
    vals = vs_ref[...]
    def logits(g):
        qn = _rms_scale(q_ref[:, g * HEAD_DIM:(g + 1) * HEAD_DIM]) * qg_ref[...]
        return _nt_dot((_rope(qn, cosq_ref[...], sinq_ref[...]) * scale).astype(BF16), keys)

    s_next = logits(0)
    for g in range(ATT_GROUP):
        s = s_next
        if g + 1 < ATT_GROUP:
            s_next = logits(g + 1)
        p = jnp.exp2(s - jnp.max(s, axis=-1, keepdims=True))
        denom = jnp.sum(p, axis=-1, keepdims=True)
        pv = jnp.dot(p.astype(BF16), vals, preferred_element_type=F32)
        o_ref[:, g * HEAD_DIM:(g + 1) * HEAD_DIM] = (pv / denom).astype(o_ref.dtype)


def _attention(proj, seq_len, layer, q_gain, k_gain, cos, sin):
    t = proj.shape[0]
    tq = min(ATT_Q_TILE, seq_len)
    nq = seq_len // tq
    qw = ATT_GROUP * HEAD_DIM
    kv = lambda off: pl.BlockSpec((seq_len, HEAD_DIM), lambda s, h, i, off=off: (s, off // HEAD_DIM + h))
    gain = pl.BlockSpec((None, 1, HEAD_DIM), lambda s, h, i: (layer, 0, 0))
    table_q = pl.BlockSpec((tq, HEAD_DIM), lambda s, h, i: (i, 0))
    table_k = pl.BlockSpec((seq_len, HEAD_DIM), lambda s, h, i: (0, 0))
    return pl.pallas_call(
        _attn_kernel,
        grid=(t // seq_len, ATT_KV_HEADS, nq),
        in_specs=[pl.BlockSpec((tq, qw), lambda s, h, i: (s * nq + i, OFF_ATT_Q // qw + h)),
                  kv(OFF_ATT_K), kv(OFF_ATT_V), table_q, table_q, table_k, table_k, gain, gain],
        out_specs=pl.BlockSpec((tq, qw), lambda s, h, i: (s * nq + i, h)),
        out_shape=jax.ShapeDtypeStruct((t, ATT_WIDTH), BF16),
        scratch_shapes=[pltpu.VMEM((seq_len, HEAD_DIM), BF16), pltpu.VMEM((seq_len, HEAD_DIM), BF16)],
        compiler_params=_params(("parallel", "parallel", "arbitrary"), 40),
        name="attention",
    )(proj, proj, proj, cos, sin, cos, sin, q_gain, k_gain)


def _merge_kernel(h_ref, ys_ref, yh_ref, ya_ref, g_ref, wb_ref, wo_ref, o_ref):
    acc = h_ref[...]
    ys, yh, ya = ys_ref[...], yh_ref[...], ya_ref[...]
    hg0, att0 = SSM_WIDTH, SSM_WIDTH + HG_WIDTH
    for c in range(D_MODEL // MERGE_TILE):
        cols = slice(c * MERGE_TILE, (c + 1) * MERGE_TILE)
        gate = lambda b: g_ref[:, b * D_MODEL + c * MERGE_TILE:b * D_MODEL + (c + 1) * MERGE_TILE]
        merged = (gate(0) * jnp.dot(ys, wb_ref[0:hg0, cols], preferred_element_type=F32)
                  + gate(1) * jnp.dot(yh, wb_ref[hg0:att0, cols], preferred_element_type=F32)
                  + gate(2) * jnp.dot(ya, wb_ref[att0:, cols], preferred_element_type=F32))
        acc = acc + jnp.dot(merged.astype(BF16), wo_ref[cols, :], preferred_element_type=F32)
    o_ref[...] = acc


def _merge(h, gates, y_ssm, y_hg, y_att, layer, w_branch, w_out):
    t, d = h.shape
    tm = min(TOKEN_TILE, t)
    rows = lambda width: pl.BlockSpec((tm, width), lambda i: (i, 0))
    weight = _resident((None, d, d), lambda i: (layer, 0, 0))
    return pl.pallas_call(
        _merge_kernel,
        grid=(t // tm,),
        in_specs=[rows(d), rows(SSM_WIDTH), rows(HG_WIDTH), rows(ATT_WIDTH), rows(GATE_WIDTH),
                  weight, weight],
        out_specs=rows(d),
        out_shape=jax.ShapeDtypeStruct((t, d), F32),
        compiler_params=_params(("parallel",), 60),
        name="merge",
    )(h, y_ssm, y_hg, y_att, gates, w_branch, w_out)


def _trunk(x, p):
    n_seq, seq_len, d = x.shape
    cos, sin = _rope_tables(seq_len)
    h = x.reshape(n_seq * seq_len, d)
    for layer in range(DEPTH):
        h, u = _ffn(h, layer, p["ffn1_norm"], p["ffn1_w_gate_up"], p["ffn1_w_down"], p["mix_norm"])
        proj = _in_proj(u, layer, p["w_in"], 0, MAIN_WIDTH, False)
        gates = _in_proj(u, layer, p["w_in"], MAIN_WIDTH, GATE_WIDTH, True)
        y_ssm = _s5(proj, seq_len, layer, p["s5_w1"], p["s5_w2"], p["s5_pw"], p["ssm_d"], p["ssm_w_glu"])
        y_hg = _hgrn(proj, seq_len, layer, p["hg_lb"], p["hg_out_norm"])
        y_att = _attention(proj, seq_len, layer, p["att_q_norm"], p["att_k_norm"], cos, sin)
        h = _merge(h, gates, y_ssm, y_hg, y_att, layer, p["w_branch"], p["w_out"])
        h = _ffn(h, layer, p["ffn2_norm"], p["ffn2_w_gate_up"], p["ffn2_w_down"])
    return h.reshape(n_seq, seq_len, d)


def kernel(x_prompt, x_sample, ffn1_norm, ffn1_w_gate_up, ffn1_w_down, mix_norm, w_in, ssm_lambda_re, ssm_lambda_im, ssm_log_dt, ssm_b_re, ssm_b_im, ssm_c_re, ssm_c_im, ssm_d, ssm_w_glu, hg_lb_logits, hg_out_norm, att_q_norm, att_k_norm, w_branch, w_out, ffn2_norm, ffn2_w_gate_up, ffn2_w_down):
    blocks_per_seq = x_prompt.shape[1] // S5_BLOCK
    assert x_sample.shape[1] == x_prompt.shape[1] and blocks_per_seq & (blocks_per_seq - 1) == 0
    s5_w1, s5_w2, s5_pw = _s5_prepare(ssm_lambda_re, ssm_lambda_im, ssm_log_dt, ssm_b_re, ssm_b_im,
                                      ssm_c_re, ssm_c_im, blocks_per_seq.bit_length() - 1)
    row = lambda a: a.astype(F32).reshape(DEPTH, 1, a.shape[-1])
    bf16 = lambda a: a.astype(BF16)
    p = dict(
        ffn1_norm=row(ffn1_norm), ffn1_w_gate_up=bf16(ffn1_w_gate_up), ffn1_w_down=bf16(ffn1_w_down),
        mix_norm=row(mix_norm), w_in=bf16(w_in),
        s5_w1=s5_w1, s5_w2=s5_w2, s5_pw=s5_pw, ssm_d=row(ssm_d), ssm_w_glu=bf16(ssm_w_glu),
        hg_lb=_hg_lower_bound(hg_lb_logits), hg_out_norm=row(hg_out_norm),
        att_q_norm=row(att_q_norm), att_k_norm=row(att_k_norm),
        w_branch=bf16(w_branch), w_out=bf16(w_out),
        ffn2_norm=row(ffn2_norm), ffn2_w_gate_up=bf16(ffn2_w_gate_up), ffn2_w_down=bf16(ffn2_w_down),
    )
    return _trunk(x_prompt, p), _trunk(x_sample, p)
```

```python
import functools
import math

import jax
import jax.numpy as jnp
from jax import lax
from jax.experimental import pallas as pl
from jax.experimental.pallas import tpu as pltpu

F32 = jnp.float32
BF16 = jnp.bfloat16

D_MODEL = 2048
DEPTH = 4
GRID_W = 64
EPS = 1e-6
LB_FLOOR = 1e-30
SSM_WIDTH = 512
SSM_GROUP = 16
SSM_GROUPS = 32
SSM_STATE = 64
SSM_STATES = SSM_GROUPS * SSM_STATE
HG_HEAD_DIM = 128
HG_WIDTH = 512
HG_HEADS = 4
HEAD_DIM = 128
ATT_Q_HEADS = 8
ATT_KV_HEADS = 2
ATT_GROUP = ATT_Q_HEADS // ATT_KV_HEADS
ATT_WIDTH = 1024
KV_WIDTH = 256
ROPE_BASE = 10000.0
ROPE_AXIS_DIM = HEAD_DIM // 2
N_BRANCH = 3
D_FF = 5632
GATE_WIDTH = N_BRANCH * D_MODEL
MAIN_WIDTH = SSM_WIDTH + 5 * HG_WIDTH + ATT_WIDTH + 2 * KV_WIDTH

OFF_SSM = 0
OFF_HG_Q = 512
OFF_HG_FF = 1024
OFF_HG_FB = 1536
OFF_HG_I = 2048
OFF_HG_G = 2560
OFF_ATT_Q = 3072
OFF_ATT_K = 4096
OFF_ATT_V = 4352

LANES = 128
SLABS = SSM_STATES // LANES
S5_HALVES = 2
HALF_WIDTH = SSM_WIDTH // S5_HALVES
HALF_STATES = SSM_STATES // S5_HALVES
HALF_SLABS = SLABS // S5_HALVES

TOKEN_TILE = 512
FF_TILE = 512
PROJ_ROWS = 1024
PROJ_COLS = 1536
MERGE_TILE = 512
ATT_Q_TILE = 256
S5_CHUNK = 256
S5_PITCH = S5_CHUNK + 4
HG_CHUNK = 128
HG_SUB = 16
HG_HEADS_PER_STEP = 2

MIB = 1024 * 1024


def _params(semantics, vmem_mib):
    return pltpu.CompilerParams(dimension_semantics=semantics, vmem_limit_bytes=vmem_mib * MIB)


def _resident(shape, index_map):
    return pl.BlockSpec(shape, index_map, pipeline_mode=pl.Buffered(1))


def _rms_scale(x):
    return x * lax.rsqrt(jnp.mean(x * x, axis=-1, keepdims=True) + EPS)


def _nt_dot(a, b):
    return lax.dot_general(a, b, (((1,), (1,)), ((), ())), preferred_element_type=F32)


def _ffn_kernel(x_ref, g_ref, wg_ref, wu_ref, wd_ref, *rest, emit_normed):
    if emit_normed:
        g2_ref, o_ref, u_ref, xn_ref, acc_ref = rest
    else:
        o_ref, xn_ref, acc_ref = rest
    f = pl.program_id(1)

    @pl.when(f == 0)
    def _():
        xn_ref[...] = (_rms_scale(x_ref[...]) * g_ref[...]).astype(BF16)
        acc_ref[...] = jnp.zeros_like(acc_ref)

    xn = xn_ref[...]
    half = FF_TILE // 2
    acts = []
    for s in range(2):
        cols = slice(s * half, (s + 1) * half)
        gate = jnp.dot(xn, wg_ref[:, cols], preferred_element_type=F32)
        up = jnp.dot(xn, wu_ref[:, cols], preferred_element_type=F32)
        acts.append((gate * jax.nn.sigmoid(gate) * up).astype(BF16))
    acc_ref[...] += (jnp.dot(acts[0], wd_ref[0:half, :], preferred_element_type=F32)
                     + jnp.dot(acts[1], wd_ref[half:, :], preferred_element_type=F32))

    @pl.when(f == pl.num_programs(1) - 1)
    def _():
        out = x_ref[...] + 0.5 * acc_ref[...]
        o_ref[...] = out
        if emit_normed:
            u_ref[...] = (_rms_scale(out) * g2_ref[...]).astype(BF16)


def _ffn(x, layer, gain, w_gate_up, w_down, next_gain=None):
    t, d = x.shape
    tm = min(TOKEN_TILE, t)
    nf = D_FF // FF_TILE
    emit = next_gain is not None
    rows = pl.BlockSpec((tm, d), lambda i, f: (i, 0))
    gain_spec = pl.BlockSpec((None, 1, d), lambda i, f: (layer, 0, 0))
    in_specs = [
        rows,
        gain_spec,
        pl.BlockSpec((None, d, FF_TILE), lambda i, f: (layer, 0, f)),
        pl.BlockSpec((None, d, FF_TILE), lambda i, f: (layer, 0, f + nf)),
        pl.BlockSpec((None, FF_TILE, d), lambda i, f: (layer, f, 0)),
    ]
    args = [x, gain, w_gate_up, w_gate_up, w_down]
    out_specs = [rows]
    out_shape = [jax.ShapeDtypeStruct((t, d), F32)]
    if emit:
        in_specs.append(gain_spec)
        args.append(next_gain)
        out_specs.append(rows)
        out_shape.append(jax.ShapeDtypeStruct((t, d), BF16))
    outs = pl.pallas_call(
        functools.partial(_ffn_kernel, emit_normed=emit),
        grid=(t // tm, nf),
        in_specs=in_specs,
        out_specs=out_specs,
        out_shape=out_shape,
        scratch_shapes=[pltpu.VMEM((tm, d), BF16), pltpu.VMEM((tm, d), F32)],
        compiler_params=_params(("parallel", "arbitrary"), 52),
        name="ffn_norm" if emit else "ffn",
    )(*args)
    return outs if emit else outs[0]


def _proj_kernel(a_ref, w_ref, o_ref, *, squash):
    acc = jnp.dot(a_ref[...], w_ref[...], preferred_element_type=F32)
    o_ref[...] = (jax.nn.sigmoid(acc) if squash else acc).astype(o_ref.dtype)


def _in_proj(u, layer, w_in, first_col, width, squash):
    t, d = u.shape
    tm = min(PROJ_ROWS, t)
    col0 = first_col // PROJ_COLS
    return pl.pallas_call(
        functools.partial(_proj_kernel, squash=squash),
        grid=(t // tm, width // PROJ_COLS),
        in_specs=[pl.BlockSpec((tm, d), lambda i, j: (i, 0)),
                  pl.BlockSpec((None, d, PROJ_COLS), lambda i, j: (layer, 0, col0 + j))],
        out_specs=pl.BlockSpec((tm, PROJ_COLS), lambda i, j: (i, j)),
        out_shape=jax.ShapeDtypeStruct((t, width), BF16 if squash else F32),
        compiler_params=_params(("parallel", "arbitrary"), 40),
        name="gate_proj" if squash else "in_proj",
    )(u, w_in)


def _lower_bound_kernel(logit_ref, lb_ref):
    x = logit_ref[...]
    e = jnp.exp(x - jnp.max(x, axis=0, keepdims=True))
    p = e / jnp.sum(e, axis=0, keepdims=True)
    run = jnp.zeros_like(p[0:1])
    for layer in range(DEPTH):
        run = run + p[layer:layer + 1]
        lb_ref[layer:layer + 1, :] = jnp.clip(run - p[0:1], 0.0, 1.0 - 1e-6)


def _hg_lower_bound(logits):
    flat = logits.astype(F32).reshape(DEPTH, 2 * HG_WIDTH)
    lb = pl.pallas_call(
        _lower_bound_kernel,
        out_shape=jax.ShapeDtypeStruct(flat.shape, F32),
        name="hg_lower_bound",
    )(flat)
    return lb.reshape(DEPTH, 2, HG_WIDTH)


def _s5_prep_kernel(lre_ref, lim_ref, ldt_ref, bre_ref, bim_ref, a_ref, bbar_ref):
    lre = lre_ref[0]
    lim = lim_ref[0]
    dt = jnp.exp(ldt_ref[0])
    mag = jnp.exp(lre * dt)
    a_re = mag * jnp.cos(lim * dt)
    a_im = mag * jnp.sin(lim * dt)
    den = lre * lre + lim * lim
    num_re = a_re - 1.0
    coef_re = (num_re * lre + a_im * lim) / den
    coef_im = (a_im * lre - num_re * lim) / den
    a_ref[0, 0:1, :] = a_re
    a_ref[0, 1:2, :] = a_im
    bre = bre_ref[0]
    bim = bim_ref[0]
    bbar_ref[0, :, 0:HALF_STATES] = (coef_re * bre - coef_im * bim).astype(BF16)
    bbar_ref[0, :, HALF_STATES:] = (coef_re * bim + coef_im * bre).astype(BF16)


def _block_diag_in(b):
    n = b.shape[0]
    per = SSM_GROUPS // S5_HALVES
    b = b.reshape(n * S5_HALVES, per, SSM_STATE, SSM_GROUP)
    full = jnp.einsum('xgnp,gh->xgphn', b, jnp.eye(per, dtype=b.dtype))
    return full.reshape(n * S5_HALVES, HALF_WIDTH, HALF_STATES)


def _block_diag_out(c):
    n = c.shape[0]
    per = SSM_GROUPS // S5_HALVES
    c = c.reshape(n * S5_HALVES, per, SSM_GROUP, SSM_STATE)
    full = jnp.einsum('xgpn,gh->xgnhp', c, jnp.eye(per, dtype=c.dtype))
    return full.reshape(n * S5_HALVES, HALF_STATES, HALF_WIDTH)


def _s5_prepare(lam_re, lam_im, log_dt, b_re, b_im, c_re, c_im):
    n = DEPTH * 2
    m = n * S5_HALVES
    flat = lambda a: a.astype(F32).reshape(m, 1, HALF_STATES)
    ldt = jnp.repeat(log_dt.astype(F32).reshape(n, SSM_GROUPS), SSM_STATE, axis=-1)
    bre = _block_diag_in(b_re.astype(F32).reshape(n, SSM_GROUPS, SSM_STATE, SSM_GROUP))
    bim = _block_diag_in(b_im.astype(F32).reshape(n, SSM_GROUPS, SSM_STATE, SSM_GROUP))
    vec = pl.BlockSpec((1, 1, HALF_STATES), lambda i: (i, 0, 0))
    mat = pl.BlockSpec((1, HALF_WIDTH, HALF_STATES), lambda i: (i, 0, 0))
    a, bbar = pl.pallas_call(
        _s5_prep_kernel,
        grid=(m,),
        in_specs=[vec, vec, vec, mat, mat],
        out_specs=[pl.BlockSpec((1, 2, HALF_STATES), lambda i: (i, 0, 0)),
                   pl.BlockSpec((1, HALF_WIDTH, 2 * HALF_STATES), lambda i: (i, 0, 0))],
        out_shape=[jax.ShapeDtypeStruct((m, 2, HALF_STATES), F32),
                   jax.ShapeDtypeStruct((m, HALF_WIDTH, 2 * HALF_STATES), BF16)],
        compiler_params=_params(("parallel",), 32),
        name="s5_prep",
    )(flat(lam_re), flat(lam_im), flat(ldt), bre, bim)
    a = a.reshape(DEPTH, 2, S5_HALVES, 2, HALF_STATES).transpose(0, 1, 3, 2, 4)
    a = a.reshape(DEPTH, 2, 2, SLABS, LANES)
    bbar = bbar.reshape(DEPTH, 2, S5_HALVES, HALF_WIDTH, 2 * HALF_STATES)
    cre = _block_diag_out(c_re.astype(F32).reshape(n, SSM_GROUPS, SSM_GROUP, SSM_STATE))
    cim = _block_diag_out(c_im.astype(F32).reshape(n, SSM_GROUPS, SSM_GROUP, SSM_STATE))
    cmat = jnp.stack([cre.reshape(DEPTH, 2, S5_HALVES, HALF_STATES, HALF_WIDTH),
                      cim.reshape(DEPTH, 2, S5_HALVES, HALF_STATES, HALF_WIDTH)], axis=2)
    return a, bbar, cmat.astype(BF16)


def _s5_kernel(u_ref, a_ref, bbar_ref, c_ref, d_ref, wglu_ref, o_ref, y_ref, sf_ref, sb_ref, xf_ref, xb_ref):
    seq_len = u_ref.shape[0]
    q = min(S5_CHUNK, seq_len)
    n_chunks = seq_len // q
    im0 = SLABS * S5_PITCH
    y_ref[...] = d_ref[...] * u_ref[...]
    a_f_re, a_f_im = a_ref[0, 0], a_ref[0, 1]
    a_b_re, a_b_im = a_ref[1, 0], a_ref[1, 1]

    def slab_rows(j):
        return slice(j * S5_PITCH, j * S5_PITCH + q)

    def project_in(rows, direction, slab_ref):
        for half in range(S5_HALVES):
            u_half = u_ref[rows, half * HALF_WIDTH:(half + 1) * HALF_WIDTH].astype(BF16)
            bu = jnp.dot(u_half, bbar_ref[direction, half], preferred_element_type=F32)
            for part in range(2):
                for j in range(HALF_SLABS):
                    col = part * HALF_STATES + j * LANES
                    slab_ref[slab_rows(part * SLABS + half * HALF_SLABS + j), :] = bu[:, col:col + LANES]

    def project_out(direction, slab_ref):
        halves = []
        for half in range(S5_HALVES):
            def natural(part):
                first = part * SLABS + half * HALF_SLABS
                return jnp.concatenate([slab_ref[slab_rows(first + j), :] for j in range(HALF_SLABS)],
                                       axis=1).astype(BF16)
            halves.append(jnp.dot(natural(0), c_ref[direction, 0, half], preferred_element_type=F32)
                          - jnp.dot(natural(1), c_ref[direction, 1, half], preferred_element_type=F32))
        return jnp.concatenate(halves, axis=1)

    def advance(slab_ref, state_ref, t, a_re, a_im, x_re, x_im):
        re_rows = pl.ds(t, SLABS, stride=S5_PITCH)
        im_rows = pl.ds(im0 + t, SLABS, stride=S5_PITCH)
        n_re = a_re * x_re - a_im * x_im + slab_ref[re_rows, :]
        n_im = a_re * x_im + a_im * x_re + slab_ref[im_rows, :]
        state_ref[re_rows, :] = n_re
        state_ref[im_rows, :] = n_im
        return n_re, n_im

    def chunk(c, carry):
        rows_f = pl.ds(pl.multiple_of(c * q, q), q)
        rows_b = pl.ds(pl.multiple_of((n_chunks - 1 - c) * q, q), q)
        project_in(rows_f, 0, sf_ref)
        project_in(rows_b, 1, sb_ref)

        def step(i, st):
            f_re, f_im = advance(sf_ref, xf_ref, i, a_f_re, a_f_im, st[0], st[1])
            b_re, b_im = advance(sb_ref, xb_ref, q - 1 - i, a_b_re, a_b_im, st[2], st[3])
            return f_re, f_im, b_re, b_im

        carry = lax.fori_loop(0, q, step, carry, unroll=4)
        y_ref[rows_f, :] += project_out(0, xf_ref)
        y_ref[rows_b, :] += project_out(1, xb_ref)
        return carry

    zero = jnp.zeros((SLABS, LANES), F32)
    lax.fori_loop(0, n_chunks, chunk, (zero, zero, zero, zero))
    z = jax.nn.gelu(y_ref[...])
    glu = jnp.dot(z.astype(BF16), wglu_ref[...], preferred_element_type=F32)
    o_ref[...] = (z * jax.nn.sigmoid(glu)).astype(o_ref.dtype)


def _s5(proj, seq_len, layer, a, bbar, cmat, d_skip, w_glu):
    t = proj.shape[0]

    def whole(shape):
        return _resident((None,) + shape, lambda s: (layer,) + (0,) * len(shape))

    return pl.pallas_call(
        _s5_kernel,
        grid=(t // seq_len,),
        in_specs=[
            pl.BlockSpec((seq_len, SSM_WIDTH), lambda s: (s, OFF_SSM // SSM_WIDTH)),
            whole((2, 2, SLABS, LANES)),
            whole((2, S5_HALVES, HALF_WIDTH, 2 * HALF_STATES)),
            whole((2, 2, S5_HALVES, HALF_STATES, HALF_WIDTH)),
            whole((1, SSM_WIDTH)),
            whole((SSM_WIDTH, SSM_WIDTH)),
        ],
        out_specs=pl.BlockSpec((seq_len, SSM_WIDTH), lambda s: (s, 0)),
        out_shape=jax.ShapeDtypeStruct((t, SSM_WIDTH), BF16),
        scratch_shapes=[pltpu.VMEM((seq_len, SSM_WIDTH), F32)]
        + [pltpu.VMEM((2 * SLABS * S5_PITCH, LANES), F32)] * 4,
        compiler_params=_params(("parallel",), 56),
        name="s5",
    )(proj, a, bbar, cmat, d_skip, w_glu)


def _hg_gates(z, lb):
    e = jnp.exp(-jnp.abs(z))
    r = 1.0 / (1.0 + e)
    er = e * r
    pos = z >= 0.0
    log_f = jnp.log(jnp.maximum(lb, LB_FLOOR) + (1.0 - lb) * jnp.where(pos, r, er))
    return log_f, (1.0 - lb) * jnp.where(pos, er, r)


def _pair_diag(x):
    lane = lax.broadcasted_iota(jnp.int32, x.shape, 1)
    zero = jnp.zeros_like(x)
    return jnp.concatenate([jnp.where(lane < HG_HEAD_DIM, x, zero),
                            jnp.where(lane >= HG_HEAD_DIM, x, zero)], axis=0)


def _hg_reach(c, reverse):
    row = lax.broadcasted_iota(jnp.int32, (c, c), 0)
    col = lax.broadcasted_iota(jnp.int32, (c, c), 1)
    return (col >= row) if reverse else (col <= row)


def _hg_prepare(q_raw, z, vc, lb, reverse):
    c = q_raw.shape[0]
    qc = q_raw * jax.nn.sigmoid(q_raw)
    lfc, kc = _hg_gates(z, lb)
    ones = jnp.where(_hg_reach(c, reverse), 1.0, 0.0).astype(BF16)
    part0 = lfc.astype(BF16)
    rest = lfc - part0.astype(F32)
    part1 = rest.astype(BF16)
    part2 = (rest - part1.astype(F32)).astype(BF16)
    sums = jnp.dot(ones, jnp.concatenate([part0, part1, part2], axis=1), preferred_element_type=F32)
    w = lfc.shape[1]
    cum = sums[:, 0:w] + sums[:, w:2 * w] + sums[:, 2 * w:3 * w]
    total = cum[0:1] if reverse else cum[c - 1:c]
    q_state = (qc * jnp.exp(cum)).astype(BF16)
    k_state = (kc * jnp.exp(total - cum)).astype(BF16)
    v_t = jnp.concatenate([vc[:, 0:HG_HEAD_DIM].T, vc[:, HG_HEAD_DIM:].T], axis=1).astype(BF16)
    return qc, kc, cum, cum - lfc, q_state, k_state, v_t, vc.astype(BF16), jnp.exp(total)


def _hg_state_step(prep, state_t):
    _, _, _, _, q_state, k_state, v_t, _, decay = prep
    out = _nt_dot(q_state, _pair_diag(state_t.astype(BF16)))
    return out, state_t * decay + jnp.dot(v_t, _pair_diag(k_state), preferred_element_type=F32)


def _hg_block_operands(d, step):
    c = d["qc"].shape[0]
    n_blk = c // HG_SUB
    reverse = d["reverse"]
    blk = n_blk - 1 - step if reverse else step
    lo, hi = blk * HG_SUB, (blk + 1) * HG_SUB
    cum = d["cum"]
    ref_row = d["cum_before"][hi - 1:hi] if reverse else d["cum_before"][lo:lo + 1]
    fresh = d["kc"][lo:hi] * jnp.exp(ref_row - cum[lo:hi])
    if d["keys"] is None:
        keys = fresh
    else:
        moved = d["keys"] * jnp.exp(ref_row - d["prev_ref"])
        keys = jnp.concatenate([fresh, moved] if reverse else [moved, fresh], axis=0)
    d["keys"], d["prev_ref"] = keys, ref_row
    padded = keys
    if keys.shape[0] < c:
        blank = jnp.zeros((c - keys.shape[0], keys.shape[1]), F32)
        padded = jnp.concatenate([blank, keys] if reverse else [keys, blank], axis=0)
    q_blk = (d["qc"][lo:hi] * jnp.exp(cum[lo:hi] - ref_row)).astype(BF16)
    return blk, _pair_diag(q_blk), padded.astype(BF16)


def _hg_scores(prep_f, prep_b):
    c = prep_f[0].shape[0]
    n_blk = c // HG_SUB
    dirs = [dict(qc=p[0], kc=p[1], cum=p[2], cum_before=p[3], reverse=rev, keys=None, prev_ref=None,
                 rows=[[None] * n_blk for _ in range(HG_HEADS_PER_STEP)])
            for p, rev in ((prep_f, False), (prep_b, True))]
    for step in range(n_blk):
        blk_f, q_f, k_f = _hg_block_operands(dirs[0], step)
        blk_b, q_b, k_b = _hg_block_operands(dirs[1], step)
        res = _nt_dot(jnp.concatenate([q_f, q_b], axis=0), jnp.concatenate([k_f, k_b], axis=0))
        for head in range(HG_HEADS_PER_STEP):
            dirs[0]["rows"][head][blk_f] = res[head * HG_SUB:(head + 1) * HG_SUB, 0:c]
            dirs[1]["rows"][head][blk_b] = res[(2 + head) * HG_SUB:(3 + head) * HG_SUB, c:2 * c]
    return [jnp.concatenate(
        [jnp.where(_hg_reach(c, d["reverse"]), jnp.concatenate(rows, axis=0), 0.0).astype(BF16)
         for rows in d["rows"]], axis=1) for d in dirs]


def _hgrn_kernel(q_ref, ff_ref, fb_ref, i_ref, g_ref, lb_ref, gain_ref, o_ref, acc_ref):
    seq_len = q_ref.shape[0]
    c = HG_CHUNK
    n_chunks = seq_len // c
    width = HG_HEADS_PER_STEP * HG_HEAD_DIM
    acc_ref[...] = jnp.zeros_like(acc_ref)

    def rows_of(ci):
        return pl.ds(pl.multiple_of(ci * c, c), c)

    def prepare(ci):
        rows_f, rows_b = rows_of(ci), rows_of(n_chunks - 1 - ci)
        return (_hg_prepare(q_ref[rows_f, :], ff_ref[rows_f, :], i_ref[rows_f, :], lb_ref[0:1, :], False),
                _hg_prepare(q_ref[rows_b, :], fb_ref[rows_b, :], i_ref[rows_b, :], lb_ref[1:2, :], True))

    def finish(ci, pending):
        for rows, (out, scores, vb) in zip((rows_of(ci), rows_of(n_chunks - 1 - ci)), pending):
            acc_ref[rows, :] += out + jnp.dot(scores, _pair_diag(vb), preferred_element_type=F32)

    def chunk(ci, carry):
        prep_f, prep_b, state_f, state_b, pending = carry
        finish(jnp.maximum(ci - 1, 0), pending)
        out_f, state_f = _hg_state_step(prep_f, state_f)
        out_b, state_b = _hg_state_step(prep_b, state_b)
        scores_f, scores_b = _hg_scores(prep_f, prep_b)
        pending = ((out_f, scores_f, prep_f[7]), (out_b, scores_b, prep_b[7]))
        next_f, next_b = prepare(jnp.minimum(ci + 1, n_chunks - 1))
        return next_f, next_b, state_f, state_b, pending

    first_f, first_b = prepare(0)
    state0 = jnp.zeros((HG_HEAD_DIM, width), F32)
    idle = (jnp.zeros((c, width), F32), jnp.zeros((c, width), BF16), jnp.zeros((c, width), BF16))
    carry = lax.fori_loop(0, n_chunks, chunk, (first_f, first_b, state0, state0, (idle, idle)))
    finish(n_chunks - 1, carry[4])
    for head in range(HG_HEADS_PER_STEP):
        lanes = slice(head * HG_HEAD_DIM, (head + 1) * HG_HEAD_DIM)
        g_raw = g_ref[:, lanes]
        normed = _rms_scale(acc_ref[:, lanes]) * gain_ref[...]
        o_ref[:, lanes] = (normed * (g_raw * jax.nn.sigmoid(g_raw))).astype(o_ref.dtype)


def _hgrn(proj, seq_len, layer, lower_bound, out_gain):
    t = proj.shape[0]
    width = HG_HEADS_PER_STEP * HG_HEAD_DIM
    col = lambda off: pl.BlockSpec((seq_len, width), lambda s, h, off=off: (s, off // width + h))
    return pl.pallas_call(
        _hgrn_kernel,
        grid=(t // seq_len, HG_HEADS // HG_HEADS_PER_STEP),
        in_specs=[col(OFF_HG_Q), col(OFF_HG_FF), col(OFF_HG_FB), col(OFF_HG_I), col(OFF_HG_G),
                  pl.BlockSpec((None, 2, width), lambda s, h: (layer, 0, h)),
                  pl.BlockSpec((None, 1, HG_HEAD_DIM), lambda s, h: (layer, 0, 0))],
        out_specs=pl.BlockSpec((seq_len, width), lambda s, h: (s, h)),
        out_shape=jax.ShapeDtypeStruct((t, HG_WIDTH), BF16),
        scratch_shapes=[pltpu.VMEM((seq_len, width), F32)],
        compiler_params=_params(("parallel", "parallel"), 40),
        name="hgrn2",
    )(proj, proj, proj, proj, proj, lower_bound, out_gain)


def _rope_tables(seq_len):
    pos = jnp.arange(seq_len, dtype=jnp.int32)
    row = (pos // GRID_W).astype(F32)
    colp = (pos % GRID_W).astype(F32)
    inv_freq = ROPE_BASE ** (-jnp.arange(0, ROPE_AXIS_DIM, 2, dtype=F32) / ROPE_AXIS_DIM)
    ang_r = row[:, None] * inv_freq[None, :]
    ang_c = colp[:, None] * inv_freq[None, :]
    cos = jnp.concatenate([jnp.cos(ang_r)] * 2 + [jnp.cos(ang_c)] * 2, axis=-1)
    sin = jnp.concatenate([-jnp.sin(ang_r), jnp.sin(ang_r), -jnp.sin(ang_c), jnp.sin(ang_c)], axis=-1)
    return cos, sin


def _rope(x, cos, sin):
    half = ROPE_AXIS_DIM // 2
    lane = lax.broadcasted_iota(jnp.int32, x.shape, 1)
    partner = jnp.where(lane % ROPE_AXIS_DIM < half,
                        pltpu.roll(x, HEAD_DIM - half, axis=1), pltpu.roll(x, half, axis=1))
    return x * cos + partner * sin


def _attn_kernel(q_ref, k_ref, v_ref, cosq_ref, sinq_ref, cosk_ref, sink_ref, qg_ref, kg_ref,
                 o_ref, ks_ref, vs_ref):
    @pl.when(pl.program_id(2) == 0)
    def _():
        kn = _rms_scale(k_ref[...]) * kg_ref[...]
        ks_ref[...] = _rope(kn, cosk_ref[...], sink_ref[...]).astype(BF16)
        vs_ref[...] = v_ref[...].astype(BF16)

    scale = HEAD_DIM ** -0.5 * math.log2(math.e)
    keys = ks_ref[...]
    vals = vs_ref[...]

    def logits(g):
        qn = _rms_scale(q_ref[:, g * HEAD_DIM:(g + 1) * HEAD_DIM]) * qg_ref[...]
        return _nt_dot((_rope(qn, cosq_ref[...], sinq_ref[...]) * scale).astype(BF16), keys)

    s_next = logits(0)
    for g in range(ATT_GROUP):
        s = s_next
        if g + 1 < ATT_GROUP:
            s_next = logits(g + 1)
        p = jnp.exp2(s - jnp.max(s, axis=-1, keepdims=True))
        denom = jnp.sum(p, axis=-1, keepdims=True)
        pv = jnp.dot(p.astype(BF16), vals, preferred_element_type=F32)
        o_ref[:, g * HEAD_DIM:(g + 1) * HEAD_DIM] = (pv / denom).astype(o_ref.dtype)


def _attention(proj, seq_len, layer, q_gain, k_gain, cos, sin):
    t = proj.shape[0]
    tq = min(ATT_Q_TILE, seq_len)
    nq = seq_len // tq
    qw = ATT_GROUP * HEAD_DIM
    kv = lambda off: pl.BlockSpec((seq_len, HEAD_DIM), lambda s, h, i, off=off: (s, off // HEAD_DIM + h))
    gain = pl.BlockSpec((None, 1, HEAD_DIM), lambda s, h, i: (layer, 0, 0))
    table_q = pl.BlockSpec((tq, HEAD_DIM), lambda s, h, i: (i, 0))
    table_k = pl.BlockSpec((seq_len, HEAD_DIM), lambda s, h, i: (0, 0))
    return pl.pallas_call(
        _attn_kernel,
        grid=(t // seq_len, ATT_KV_HEADS, nq),
        in_specs=[pl.BlockSpec((tq, qw), lambda s, h, i: (s * nq + i, OFF_ATT_Q // qw + h)),
                  kv(OFF_ATT_K), kv(OFF_ATT_V), table_q, table_q, table_k, table_k, gain, gain],
        out_specs=pl.BlockSpec((tq, qw), lambda s, h, i: (s * nq + i, h)),
        out_shape=jax.ShapeDtypeStruct((t, ATT_WIDTH), BF16),
        scratch_shapes=[pltpu.VMEM((seq_len, HEAD_DIM), BF16), pltpu.VMEM((seq_len, HEAD_DIM), BF16)],
        compiler_params=_params(("parallel", "parallel", "arbitrary"), 40),
        name="attention",
    )(proj, proj, proj, cos, sin, cos, sin, q_gain, k_gain)


def _merge_kernel(h_ref, ys_ref, yh_ref, ya_ref, g_ref, wb_ref, wo_ref, o_ref):
    acc = h_ref[...]
    ys, yh, ya = ys_ref[...], yh_ref[...], ya_ref[...]
    hg0, att0 = SSM_WIDTH, SSM_WIDTH + HG_WIDTH
    for c in range(D_MODEL // MERGE_TILE):
        cols = slice(c * MERGE_TILE, (c + 1) * MERGE_TILE)
        gate = lambda b: g_ref[:, b * D_MODEL + c * MERGE_TILE:b * D_MODEL + (c + 1) * MERGE_TILE]
        merged = (gate(0) * jnp.dot(ys, wb_ref[0:hg0, cols], preferred_element_type=F32)
                  + gate(1) * jnp.dot(yh, wb_ref[hg0:att0, cols], preferred_element_type=F32)
                  + gate(2) * jnp.dot(ya, wb_ref[att0:, cols], preferred_element_type=F32))
        acc = acc + jnp.dot(merged.astype(BF16), wo_ref[cols, :], preferred_element_type=F32)
    o_ref[...] = acc


def _merge(h, gates, y_ssm, y_hg, y_att, layer, w_branch, w_out):
    t, d = h.shape
    tm = min(TOKEN_TILE, t)
    rows = lambda width: pl.BlockSpec((tm, width), lambda i: (i, 0))
    weight = _resident((None, d, d), lambda i: (layer, 0, 0))
    return pl.pallas_call(
        _merge_kernel,
        grid=(t // tm,),
        in_specs=[rows(d), rows(SSM_WIDTH), rows(HG_WIDTH), rows(ATT_WIDTH), rows(GATE_WIDTH),
                  weight, weight],
        out_specs=rows(d),
        out_shape=jax.ShapeDtypeStruct((t, d), F32),
        compiler_params=_params(("parallel",), 60),
        name="merge",
    )(h, y_ssm, y_hg, y_att, gates, w_branch, w_out)


def _trunk(x, p):
    n_seq, seq_len, d = x.shape
    cos, sin = _rope_tables(seq_len)
    h = x.reshape(n_seq * seq_len, d)
    for layer in range(DEPTH):
        h, u = _ffn(h, layer, p["ffn1_norm"], p["ffn1_w_gate_up"], p["ffn1_w_down"], p["mix_norm"])
        proj = _in_proj(u, layer, p["w_in"], 0, MAIN_WIDTH, False)
        gates = _in_proj(u, layer, p["w_in"], MAIN_WIDTH, GATE_WIDTH, True)
        y_ssm = _s5(proj, seq_len, layer, p["s5_a"], p["s5_bbar"], p["s5_c"], p["ssm_d"], p["ssm_w_glu"])
        y_hg = _hgrn(proj, seq_len, layer, p["hg_lb"], p["hg_out_norm"])
        y_att = _attention(proj, seq_len, layer, p["att_q_norm"], p["att_k_norm"], cos, sin)
        h = _merge(h, gates, y_ssm, y_hg, y_att, layer, p["w_branch"], p["w_out"])
        h = _ffn(h, layer, p["ffn2_norm"], p["ffn2_w_gate_up"], p["ffn2_w_down"])
    return h.reshape(n_seq, seq_len, d)


def kernel(x_prompt, x_sample, ffn1_norm, ffn1_w_gate_up, ffn1_w_down, mix_norm, w_in, ssm_lambda_re, ssm_lambda_im, ssm_log_dt, ssm_b_re, ssm_b_im, ssm_c_re, ssm_c_im, ssm_d, ssm_w_glu, hg_lb_logits, hg_out_norm, att_q_norm, att_k_norm, w_branch, w_out, ffn2_norm, ffn2_w_gate_up, ffn2_w_down):
    s5_a, s5_bbar, s5_c = _s5_prepare(ssm_lambda_re, ssm_lambda_im, ssm_log_dt,
                                      ssm_b_re, ssm_b_im, ssm_c_re, ssm_c_im)
    row = lambda a: a.astype(F32).reshape(DEPTH, 1, a.shape[-1])
    bf16 = lambda a: a.astype(BF16)
    p = dict(
        ffn1_norm=row(ffn1_norm), ffn1_w_gate_up=bf16(ffn1_w_gate_up), ffn1_w_down=bf16(ffn1_w_down),
        mix_norm=row(mix_norm), w_in=bf16(w_in),
        s5_a=s5_a, s5_bbar=s5_bbar, s5_c=s5_c, ssm_d=row(ssm_d), ssm_w_glu=bf16(ssm_w_glu),
        hg_lb=_hg_lower_bound(hg_lb_logits), hg_out_norm=row(hg_out_norm),
        att_q_norm=row(att_q_norm), att_k_norm=row(att_k_norm),
        w_branch=bf16(w_branch), w_out=bf16(w_out),
        ffn2_norm=row(ffn2_norm), ffn2_w_gate_up=bf16(ffn2_w_gate_up), ffn2_w_down=bf16(ffn2_w_down),
    )
    return _trunk(x_prompt, p), _trunk(x_sample, p)
```

```python
import functools
import math

import jax
import jax.numpy as jnp
from jax import lax
from jax.experimental import pallas as pl
from jax.experimental.pallas import tpu as pltpu

F32 = jnp.float32
BF16 = jnp.bfloat16

D_MODEL = 2048
DEPTH = 4
GRID_W = 64
EPS = 1e-6
LB_FLOOR = 1e-30
SSM_WIDTH = 512
SSM_GROUP = 16
SSM_GROUPS = 32
SSM_STATE = 64
SSM_STATES = SSM_GROUPS * SSM_STATE
HG_HEAD_DIM = 128
HG_WIDTH = 512
HG_HEADS = 4
HEAD_DIM = 128
ATT_Q_HEADS = 8
ATT_KV_HEADS = 2
ATT_GROUP = ATT_Q_HEADS // ATT_KV_HEADS
ATT_WIDTH = 1024
KV_WIDTH = 256
ROPE_BASE = 10000.0
ROPE_AXIS_DIM = HEAD_DIM // 2
N_BRANCH = 3
D_FF = 5632
GATE_WIDTH = N_BRANCH * D_MODEL
MAIN_WIDTH = SSM_WIDTH + 5 * HG_WIDTH + ATT_WIDTH + 2 * KV_WIDTH

OFF_SSM = 0
OFF_HG_Q = 512
OFF_HG_FF = 1024
OFF_HG_FB = 1536
OFF_HG_I = 2048
OFF_HG_G = 2560
OFF_ATT_Q = 3072
OFF_ATT_K = 4096
OFF_ATT_V = 4352

LANES = 128
SLABS = SSM_STATES // LANES
S5_HALVES = 2
HALF_WIDTH = SSM_WIDTH // S5_HALVES
HALF_STATES = SSM_STATES // S5_HALVES
HALF_SLABS = SLABS // S5_HALVES

TOKEN_TILE = 512
FF_TILE = 512
PROJ_ROWS = 1024
PROJ_COLS = 1536
MERGE_TILE = 512
ATT_Q_TILE = 256
S5_CHUNK = 256
S5_PITCH = S5_CHUNK + 4
HG_CHUNK = 128
HG_SUB = 16
HG_HEADS_PER_STEP = 2

MIB = 1024 * 1024


def _params(semantics, vmem_mib):
    return pltpu.CompilerParams(dimension_semantics=semantics, vmem_limit_bytes=vmem_mib * MIB)


def _resident(shape, index_map):
    return pl.BlockSpec(shape, index_map, pipeline_mode=pl.Buffered(1))


def _rms_scale(x):
    return x * lax.rsqrt(jnp.mean(x * x, axis=-1, keepdims=True) + EPS)


def _nt_dot(a, b):
    return lax.dot_general(a, b, (((1,), (1,)), ((), ())), preferred_element_type=F32)


def _ffn_kernel(x_ref, g_ref, wg_ref, wu_ref, wd_ref, *rest, emit_normed):
    if emit_normed:
        g2_ref, o_ref, u_ref, xn_ref, acc_ref = rest
    else:
        o_ref, xn_ref, acc_ref = rest
    f = pl.program_id(1)

    @pl.when(f == 0)
    def _():
        xn_ref[...] = (_rms_scale(x_ref[...]) * g_ref[...]).astype(BF16)
        acc_ref[...] = jnp.zeros_like(acc_ref)

    xn = xn_ref[...]
    half = FF_TILE // 2
    acts = []
    for s in range(2):
        cols = slice(s * half, (s + 1) * half)
        gate = jnp.dot(xn, wg_ref[:, cols], preferred_element_type=F32)
        up = jnp.dot(xn, wu_ref[:, cols], preferred_element_type=F32)
        acts.append((gate * jax.nn.sigmoid(gate) * up).astype(BF16))
    acc_ref[...] += (jnp.dot(acts[0], wd_ref[0:half, :], preferred_element_type=F32)
                     + jnp.dot(acts[1], wd_ref[half:, :], preferred_element_type=F32))

    @pl.when(f == pl.num_programs(1) - 1)
    def _():
        out = x_ref[...] + 0.5 * acc_ref[...]
        o_ref[...] = out
        if emit_normed:
            u_ref[...] = (_rms_scale(out) * g2_ref[...]).astype(BF16)


def _ffn(x, layer, gain, w_gate_up, w_down, next_gain=None):
    t, d = x.shape
    tm = min(TOKEN_TILE, t)
    nf = D_FF // FF_TILE
    emit = next_gain is not None
    rows = pl.BlockSpec((tm, d), lambda i, f: (i, 0))
    gain_spec = pl.BlockSpec((None, 1, d), lambda i, f: (layer, 0, 0))
    in_specs = [
        rows,
        gain_spec,
        pl.BlockSpec((None, d, FF_TILE), lambda i, f: (layer, 0, f)),
        pl.BlockSpec((None, d, FF_TILE), lambda i, f: (layer, 0, f + nf)),
        pl.BlockSpec((None, FF_TILE, d), lambda i, f: (layer, f, 0)),
    ]
    args = [x, gain, w_gate_up, w_gate_up, w_down]
    out_specs = [rows]
    out_shape = [jax.ShapeDtypeStruct((t, d), F32)]
    if emit:
        in_specs.append(gain_spec)
        args.append(next_gain)
        out_specs.append(rows)
        out_shape.append(jax.ShapeDtypeStruct((t, d), BF16))
    outs = pl.pallas_call(
        functools.partial(_ffn_kernel, emit_normed=emit),
        grid=(t // tm, nf),
        in_specs=in_specs,
        out_specs=out_specs,
        out_shape=out_shape,
        scratch_shapes=[pltpu.VMEM((tm, d), BF16), pltpu.VMEM((tm, d), F32)],
        compiler_params=_params(("parallel", "arbitrary"), 52),
        name="ffn_norm" if emit else "ffn",
    )(*args)
    return outs if emit else outs[0]


def _proj_kernel(a_ref, w_ref, o_ref, *, squash):
    acc = jnp.dot(a_ref[...], w_ref[...], preferred_element_type=F32)
    o_ref[...] = (jax.nn.sigmoid(acc) if squash else acc).astype(o_ref.dtype)


def _in_proj(u, layer, w_in, first_col, width, squash):
    t, d = u.shape
    tm = min(PROJ_ROWS, t)
    col0 = first_col // PROJ_COLS
    return pl.pallas_call(
        functools.partial(_proj_kernel, squash=squash),
        grid=(t // tm, width // PROJ_COLS),
        in_specs=[pl.BlockSpec((tm, d), lambda i, j: (i, 0)),
                  pl.BlockSpec((None, d, PROJ_COLS), lambda i, j: (layer, 0, col0 + j))],
        out_specs=pl.BlockSpec((tm, PROJ_COLS), lambda i, j: (i, j)),
        out_shape=jax.ShapeDtypeStruct((t, width), BF16 if squash else F32),
        compiler_params=_params(("parallel", "arbitrary"), 40),
        name="gate_proj" if squash else "in_proj",
    )(u, w_in)


def _lower_bound_kernel(logit_ref, lb_ref):
    x = logit_ref[...]
    e = jnp.exp(x - jnp.max(x, axis=0, keepdims=True))
    p = e / jnp.sum(e, axis=0, keepdims=True)
    run = jnp.zeros_like(p[0:1])
    for layer in range(DEPTH):
        run = run + p[layer:layer + 1]
        lb_ref[layer:layer + 1, :] = jnp.clip(run - p[0:1], 0.0, 1.0 - 1e-6)


def _hg_lower_bound(logits):
    flat = logits.astype(F32).reshape(DEPTH, 2 * HG_WIDTH)
    lb = pl.pallas_call(
        _lower_bound_kernel,
        out_shape=jax.ShapeDtypeStruct(flat.shape, F32),
        name="hg_lower_bound",
    )(flat)
    return lb.reshape(DEPTH, 2, HG_WIDTH)


def _s5_prep_kernel(lre_ref, lim_ref, ldt_ref, bre_ref, bim_ref, a_ref, bbar_ref):
    lre = lre_ref[0]
    lim = lim_ref[0]
    dt = jnp.exp(ldt_ref[0])
    mag = jnp.exp(lre * dt)
    a_re = mag * jnp.cos(lim * dt)
    a_im = mag * jnp.sin(lim * dt)
    den = lre * lre + lim * lim
    num_re = a_re - 1.0
    coef_re = (num_re * lre + a_im * lim) / den
    coef_im = (a_im * lre - num_re * lim) / den
    a_ref[0, 0:1, :] = a_re
    a_ref[0, 1:2, :] = a_im
    bre = bre_ref[0]
    bim = bim_ref[0]
    bbar_ref[0, :, 0:HALF_STATES] = (coef_re * bre - coef_im * bim).astype(BF16)
    bbar_ref[0, :, HALF_STATES:] = (coef_re * bim + coef_im * bre).astype(BF16)


def _block_diag_in(b):
    n = b.shape[0]
    per = SSM_GROUPS // S5_HALVES
    b = b.reshape(n * S5_HALVES, per, SSM_STATE, SSM_GROUP)
    full = jnp.einsum('xgnp,gh->xgphn', b, jnp.eye(per, dtype=b.dtype))
    return full.reshape(n * S5_HALVES, HALF_WIDTH, HALF_STATES)


def _block_diag_out(c):
    n = c.shape[0]
    per = SSM_GROUPS // S5_HALVES
    c = c.reshape(n * S5_HALVES, per, SSM_GROUP, SSM_STATE)
    full = jnp.einsum('xgpn,gh->xgnhp', c, jnp.eye(per, dtype=c.dtype))
    return full.reshape(n * S5_HALVES, HALF_STATES, HALF_WIDTH)


def _s5_prepare(lam_re, lam_im, log_dt, b_re, b_im, c_re, c_im):
    n = DEPTH * 2
    m = n * S5_HALVES
    flat = lambda a: a.astype(F32).reshape(m, 1, HALF_STATES)
    ldt = jnp.repeat(log_dt.astype(F32).reshape(n, SSM_GROUPS), SSM_STATE, axis=-1)
    bre = _block_diag_in(b_re.astype(F32).reshape(n, SSM_GROUPS, SSM_STATE, SSM_GROUP))
    bim = _block_diag_in(b_im.astype(F32).reshape(n, SSM_GROUPS, SSM_STATE, SSM_GROUP))
    vec = pl.BlockSpec((1, 1, HALF_STATES), lambda i: (i, 0, 0))
    mat = pl.BlockSpec((1, HALF_WIDTH, HALF_STATES), lambda i: (i, 0, 0))
    a, bbar = pl.pallas_call(
        _s5_prep_kernel,
        grid=(m,),
        in_specs=[vec, vec, vec, mat, mat],
        out_specs=[pl.BlockSpec((1, 2, HALF_STATES), lambda i: (i, 0, 0)),
                   pl.BlockSpec((1, HALF_WIDTH, 2 * HALF_STATES), lambda i: (i, 0, 0))],
        out_shape=[jax.ShapeDtypeStruct((m, 2, HALF_STATES), F32),
                   jax.ShapeDtypeStruct((m, HALF_WIDTH, 2 * HALF_STATES), BF16)],
        compiler_params=_params(("parallel",), 32),
        name="s5_prep",
    )(flat(lam_re), flat(lam_im), flat(ldt), bre, bim)
    a = a.reshape(DEPTH, 2, S5_HALVES, 2, HALF_STATES).transpose(0, 1, 3, 2, 4)
    a = a.reshape(DEPTH, 2, 2, SLABS, LANES)
    bbar = bbar.reshape(DEPTH, 2, S5_HALVES, HALF_WIDTH, 2 * HALF_STATES)
    cre = _block_diag_out(c_re.astype(F32).reshape(n, SSM_GROUPS, SSM_GROUP, SSM_STATE))
    cim = _block_diag_out(c_im.astype(F32).reshape(n, SSM_GROUPS, SSM_GROUP, SSM_STATE))
    cmat = jnp.stack([cre.reshape(DEPTH, 2, S5_HALVES, HALF_STATES, HALF_WIDTH),
                      cim.reshape(DEPTH, 2, S5_HALVES, HALF_STATES, HALF_WIDTH)], axis=2)
    return a, bbar, cmat.astype(BF16)


def _s5_kernel(u_ref, a_ref, bbar_ref, c_ref, d_ref, wglu_ref, o_ref, y_ref, sf_ref, sb_ref, xf_ref, xb_ref):
    seq_len = u_ref.shape[0]
    q = min(S5_CHUNK, seq_len)
    n_chunks = seq_len // q
    im0 = SLABS * S5_PITCH
    y_ref[...] = d_ref[...] * u_ref[...]
    a_f_re, a_f_im = a_ref[0, 0], a_ref[0, 1]
    a_b_re, a_b_im = a_ref[1, 0], a_ref[1, 1]

    def slab_rows(j):
        return slice(j * S5_PITCH, j * S5_PITCH + q)

    def project_in(rows, direction, slab_ref):
        for half in range(S5_HALVES):
            u_half = u_ref[rows, half * HALF_WIDTH:(half + 1) * HALF_WIDTH].astype(BF16)
            bu = jnp.dot(u_half, bbar_ref[direction, half], preferred_element_type=F32)
            for part in range(2):
                for j in range(HALF_SLABS):
                    col = part * HALF_STATES + j * LANES
                    slab_ref[slab_rows(part * SLABS + half * HALF_SLABS + j), :] = bu[:, col:col + LANES]

    def project_out(direction, slab_ref):
        halves = []
        for half in range(S5_HALVES):
            def natural(part):
                first = part * SLABS + half * HALF_SLABS
                return jnp.concatenate([slab_ref[slab_rows(first + j), :] for j in range(HALF_SLABS)],
                                       axis=1).astype(BF16)
            halves.append(jnp.dot(natural(0), c_ref[direction, 0, half], preferred_element_type=F32)
                          - jnp.dot(natural(1), c_ref[direction, 1, half], preferred_element_type=F32))
        return jnp.concatenate(halves, axis=1)

    def advance(slab_ref, state_ref, t, a_re, a_im, x_re, x_im):
        re_rows = pl.ds(t, SLABS, stride=S5_PITCH)
        im_rows = pl.ds(im0 + t, SLABS, stride=S5_PITCH)
        n_re = a_re * x_re - a_im * x_im + slab_ref[re_rows, :]
        n_im = a_re * x_im + a_im * x_re + slab_ref[im_rows, :]
        state_ref[re_rows, :] = n_re
        state_ref[im_rows, :] = n_im
        return n_re, n_im

    def chunk(c, carry):
        rows_f = pl.ds(pl.multiple_of(c * q, q), q)
        rows_b = pl.ds(pl.multiple_of((n_chunks - 1 - c) * q, q), q)
        project_in(rows_f, 0, sf_ref)
        project_in(rows_b, 1, sb_ref)

        def step(i, st):
            f_re, f_im = advance(sf_ref, xf_ref, i, a_f_re, a_f_im, st[0], st[1])
            b_re, b_im = advance(sb_ref, xb_ref, q - 1 - i, a_b_re, a_b_im, st[2], st[3])
            return f_re, f_im, b_re, b_im

        carry = lax.fori_loop(0, q, step, carry, unroll=4)
        y_ref[rows_f, :] += project_out(0, xf_ref)
        y_ref[rows_b, :] += project_out(1, xb_ref)
        return carry

    zero = jnp.zeros((SLABS, LANES), F32)
    lax.fori_loop(0, n_chunks, chunk, (zero, zero, zero, zero))
    z = jax.nn.gelu(y_ref[...])
    glu = jnp.dot(z.astype(BF16), wglu_ref[...], preferred_element_type=F32)
    o_ref[...] = (z * jax.nn.sigmoid(glu)).astype(o_ref.dtype)


def _s5(proj, seq_len, layer, a, bbar, cmat, d_skip, w_glu):
    t = proj.shape[0]

    def whole(shape):
        return _resident((None,) + shape, lambda s: (layer,) + (0,) * len(shape))

    return pl.pallas_call(
        _s5_kernel,
        grid=(t // seq_len,),
        in_specs=[
            pl.BlockSpec((seq_len, SSM_WIDTH), lambda s: (s, OFF_SSM // SSM_WIDTH)),
            whole((2, 2, SLABS, LANES)),
            whole((2, S5_HALVES, HALF_WIDTH, 2 * HALF_STATES)),
            whole((2, 2, S5_HALVES, HALF_STATES, HALF_WIDTH)),
            whole((1, SSM_WIDTH)),
            whole((SSM_WIDTH, SSM_WIDTH)),
        ],
        out_specs=pl.BlockSpec((seq_len, SSM_WIDTH), lambda s: (s, 0)),
        out_shape=jax.ShapeDtypeStruct((t, SSM_WIDTH), BF16),
        scratch_shapes=[pltpu.VMEM((seq_len, SSM_WIDTH), F32)]
        + [pltpu.VMEM((2 * SLABS * S5_PITCH, LANES), F32)] * 4,
        compiler_params=_params(("parallel",), 56),
        name="s5",
    )(proj, a, bbar, cmat, d_skip, w_glu)


def _hg_gates(z, lb):
    e = jnp.exp(-jnp.abs(z))
    r = 1.0 / (1.0 + e)
    er = e * r
    pos = z >= 0.0
    log_f = jnp.log(jnp.maximum(lb, LB_FLOOR) + (1.0 - lb) * jnp.where(pos, r, er))
    return log_f, (1.0 - lb) * jnp.where(pos, er, r)


def _pair_diag(x):
    lane = lax.broadcasted_iota(jnp.int32, x.shape, 1)
    zero = jnp.zeros_like(x)
    return jnp.concatenate([jnp.where(lane < HG_HEAD_DIM, x, zero),
                            jnp.where(lane >= HG_HEAD_DIM, x, zero)], axis=0)


def _hg_reach(c, reverse):
    row = lax.broadcasted_iota(jnp.int32, (c, c), 0)
    col = lax.broadcasted_iota(jnp.int32, (c, c), 1)
    return (col >= row) if reverse else (col <= row)


def _hg_prepare(q_raw, z, vc, lb, reverse):
    c = q_raw.shape[0]
    qc = q_raw * jax.nn.sigmoid(q_raw)
    lfc, kc = _hg_gates(z, lb)
    ones = jnp.where(_hg_reach(c, reverse), 1.0, 0.0).astype(BF16)
    part0 = lfc.astype(BF16)
    rest = lfc - part0.astype(F32)
    part1 = rest.astype(BF16)
    part2 = (rest - part1.astype(F32)).astype(BF16)
    sums = jnp.dot(ones, jnp.concatenate([part0, part1, part2], axis=1), preferred_element_type=F32)
    w = lfc.shape[1]
    cum = sums[:, 0:w] + sums[:, w:2 * w] + sums[:, 2 * w:3 * w]
    total = cum[0:1] if reverse else cum[c - 1:c]
    q_state = (qc * jnp.exp(cum)).astype(BF16)
    k_state = (kc * jnp.exp(total - cum)).astype(BF16)
    v_t = jnp.concatenate([vc[:, 0:HG_HEAD_DIM].T, vc[:, HG_HEAD_DIM:].T], axis=1).astype(BF16)
    return qc, kc, cum, cum - lfc, q_state, k_state, v_t, vc.astype(BF16), jnp.exp(total)


def _hg_state_step(prep, state_t):
    _, _, _, _, q_state, k_state, v_t, _, decay = prep
    out = _nt_dot(q_state, _pair_diag(state_t.astype(BF16)))
    return out, state_t * decay + jnp.dot(v_t, _pair_diag(k_state), preferred_element_type=F32)


def _hg_block_operands(d, step):
    c = d["qc"].shape[0]
    n_blk = c // HG_SUB
    reverse = d["reverse"]
    blk = n_blk - 1 - step if reverse else step
    lo, hi = blk * HG_SUB, (blk + 1) * HG_SUB
    cum = d["cum"]
    ref_row = d["cum_before"][hi - 1:hi] if reverse else d["cum_before"][lo:lo + 1]
    fresh = d["kc"][lo:hi] * jnp.exp(ref_row - cum[lo:hi])
    if d["keys"] is None:
        keys = fresh
    else:
        moved = d["keys"] * jnp.exp(ref_row - d["prev_ref"])
        keys = jnp.concatenate([fresh, moved] if reverse else [moved, fresh], axis=0)
    d["keys"], d["prev_ref"] = keys, ref_row
    padded = keys
    if keys.shape[0] < c:
        blank = jnp.zeros((c - keys.shape[0], keys.shape[1]), F32)
        padded = jnp.concatenate([blank, keys] if reverse else [keys, blank], axis=0)
    q_blk = (d["qc"][lo:hi] * jnp.exp(cum[lo:hi] - ref_row)).astype(BF16)
    return blk, _pair_diag(q_blk), padded.astype(BF16)


def _hg_scores(prep_f, prep_b):
    c = prep_f[0].shape[0]
    n_blk = c // HG_SUB
    dirs = [dict(qc=p[0], kc=p[1], cum=p[2], cum_before=p[3], reverse=rev, keys=None, prev_ref=None,
                 rows=[[None] * n_blk for _ in range(HG_HEADS_PER_STEP)])
            for p, rev in ((prep_f, False), (prep_b, True))]
    for step in range(n_blk):
        blk_f, q_f, k_f = _hg_block_operands(dirs[0], step)
        blk_b, q_b, k_b = _hg_block_operands(dirs[1], step)
        res = _nt_dot(jnp.concatenate([q_f, q_b], axis=0), jnp.concatenate([k_f, k_b], axis=0))
        for head in range(HG_HEADS_PER_STEP):
            dirs[0]["rows"][head][blk_f] = res[head * HG_SUB:(head + 1) * HG_SUB, 0:c]
            dirs[1]["rows"][head][blk_b] = res[(2 + head) * HG_SUB:(3 + head) * HG_SUB, c:2 * c]
    return [jnp.concatenate(
        [jnp.where(_hg_reach(c, d["reverse"]), jnp.concatenate(rows, axis=0), 0.0).astype(BF16)
         for rows in d["rows"]], axis=1) for d in dirs]


def _hgrn_kernel(q_ref, ff_ref, fb_ref, i_ref, g_ref, lb_ref, gain_ref, o_ref, acc_ref,
                 wide_a, wide_b, narrow_a, narrow_b, decay_a, decay_b):
    seq_len = q_ref.shape[0]
    c = HG_CHUNK
    n_chunks = seq_len // c
    width = HG_HEADS_PER_STEP * HG_HEAD_DIM
    acc_ref[...] = jnp.zeros_like(acc_ref)
    slots = ((wide_a, narrow_a, decay_a), (wide_b, narrow_b, decay_b))

    def rows_of(ci):
        return pl.ds(pl.multiple_of(ci * c, c), c)

    def prepare(ci, slot):
        wide, narrow, decay = slots[slot]
        rows_f, rows_b = rows_of(ci), rows_of(n_chunks - 1 - ci)
        preps = (_hg_prepare(q_ref[rows_f, :], ff_ref[rows_f, :], i_ref[rows_f, :], lb_ref[0:1, :], False),
                 _hg_prepare(q_ref[rows_b, :], fb_ref[rows_b, :], i_ref[rows_b, :], lb_ref[1:2, :], True))
        for d, prep in enumerate(preps):
            for k in range(4):
                wide[d, k] = prep[k]
                narrow[d, k] = prep[4 + k]
            decay[d] = jnp.broadcast_to(prep[8], decay.shape[1:])

    def prepared(slot, d):
        wide, narrow, decay = slots[slot]
        return (tuple(wide[d, k] for k in range(4)) + tuple(narrow[d, k] for k in range(4))
                + (decay[d, 0:1],))

    def finish(ci, slot):
        wide, narrow, _ = slots[slot]
        for d, rows in enumerate((rows_of(ci), rows_of(n_chunks - 1 - ci))):
            acc_ref[rows, :] += wide[d, 4] + jnp.dot(narrow[d, 4], _pair_diag(narrow[d, 3]),
                                                     preferred_element_type=F32)

    def middle(slot, state_f, state_b):
        wide, narrow, _ = slots[slot]
        prep_f, prep_b = prepared(slot, 0), prepared(slot, 1)
        out_f, state_f = _hg_state_step(prep_f, state_f)
        out_b, state_b = _hg_state_step(prep_b, state_b)
        scores_f, scores_b = _hg_scores(prep_f, prep_b)
        wide[0, 4], wide[1, 4] = out_f, out_b
        narrow[0, 4], narrow[1, 4] = scores_f, scores_b
        return state_f, state_b

    def pair(j, states):
        for slot in range(2):
            ci = 2 * j + slot
            states = middle(slot, *states)
            finish(ci, slot)
            prepare(jnp.minimum(ci + 2, n_chunks - 1), slot)
        return states

    prepare(0, 0)
    prepare(1, 1)
    state0 = jnp.zeros((HG_HEAD_DIM, width), F32)
    lax.fori_loop(0, n_chunks // 2, pair, (state0, state0))
    for head in range(HG_HEADS_PER_STEP):
        lanes = slice(head * HG_HEAD_DIM, (head + 1) * HG_HEAD_DIM)
        g_raw = g_ref[:, lanes]
        normed = _rms_scale(acc_ref[:, lanes]) * gain_ref[...]
        o_ref[:, lanes] = (normed * (g_raw * jax.nn.sigmoid(g_raw))).astype(o_ref.dtype)


def _hgrn(proj, seq_len, layer, lower_bound, out_gain):
    t = proj.shape[0]
    width = HG_HEADS_PER_STEP * HG_HEAD_DIM
    col = lambda off: pl.BlockSpec((seq_len, width), lambda s, h, off=off: (s, off // width + h))
    return pl.pallas_call(
        _hgrn_kernel,
        grid=(t // seq_len, HG_HEADS // HG_HEADS_PER_STEP),
        in_specs=[col(OFF_HG_Q), col(OFF_HG_FF), col(OFF_HG_FB), col(OFF_HG_I), col(OFF_HG_G),
                  pl.BlockSpec((None, 2, width), lambda s, h: (layer, 0, h)),
                  pl.BlockSpec((None, 1, HG_HEAD_DIM), lambda s, h: (layer, 0, 0))],
        out_specs=pl.BlockSpec((seq_len, width), lambda s, h: (s, h)),
        out_shape=jax.ShapeDtypeStruct((t, HG_WIDTH), BF16),
        scratch_shapes=[pltpu.VMEM((seq_len, width), F32)]
        + [pltpu.VMEM((2, 5, HG_CHUNK, width), F32)] * 2
        + [pltpu.VMEM((2, 5, HG_CHUNK, width), BF16)] * 2
        + [pltpu.VMEM((2, 8, width), F32)] * 2,
        compiler_params=_params(("parallel", "parallel"), 40),
        name="hgrn2",
    )(proj, proj, proj, proj, proj, lower_bound, out_gain)


def _rope_tables(seq_len):
    pos = jnp.arange(seq_len, dtype=jnp.int32)
    row = (pos // GRID_W).astype(F32)
    colp = (pos % GRID_W).astype(F32)
    inv_freq = ROPE_BASE ** (-jnp.arange(0, ROPE_AXIS_DIM, 2, dtype=F32) / ROPE_AXIS_DIM)
    ang_r = row[:, None] * inv_freq[None, :]
    ang_c = colp[:, None] * inv_freq[None, :]
    cos = jnp.concatenate([jnp.cos(ang_r)] * 2 + [jnp.cos(ang_c)] * 2, axis=-1)
    sin = jnp.concatenate([-jnp.sin(ang_r), jnp.sin(ang_r), -jnp.sin(ang_c), jnp.sin(ang_c)], axis=-1)
    return cos, sin


def _rope(x, cos, sin):
    half = ROPE_AXIS_DIM // 2
    lane = lax.broadcasted_iota(jnp.int32, x.shape, 1)
    partner = jnp.where(lane % ROPE_AXIS_DIM < half,
                        pltpu.roll(x, HEAD_DIM - half, axis=1), pltpu.roll(x, half, axis=1))
    return x * cos + partner * sin


def _attn_kernel(q_ref, k_ref, v_ref, cosq_ref, sinq_ref, cosk_ref, sink_ref, qg_ref, kg_ref,
                 o_ref, ks_ref, vs_ref):
    @pl.when(pl.program_id(2) == 0)
    def _():
        kn = _rms_scale(k_ref[...]) * kg_ref[...]
        ks_ref[...] = _rope(kn, cosk_ref[...], sink_ref[...]).astype(BF16)
        vs_ref[...] = v_ref[...].astype(BF16)

    scale = HEAD_DIM ** -0.5 * math.log2(math.e)
    keys = ks_ref[...]
    vals = vs_ref[...]

    def logits(g):
        qn = _rms_scale(q_ref[:, g * HEAD_DIM:(g + 1) * HEAD_DIM]) * qg_ref[...]
        return _nt_dot((_rope(qn, cosq_ref[...], sinq_ref[...]) * scale).astype(BF16), keys)

    s_next = logits(0)
    for g in range(ATT_GROUP):
        s = s_next
        if g + 1 < ATT_GROUP:
            s_next = logits(g + 1)
        p = jnp.exp2(s - jnp.max(s, axis=-1, keepdims=True))
        denom = jnp.sum(p, axis=-1, keepdims=True)
        pv = jnp.dot(p.astype(BF16), vals, preferred_element_type=F32)
        o_ref[:, g * HEAD_DIM:(g + 1) * HEAD_DIM] = (pv / denom).astype(o_ref.dtype)


def _attention(proj, seq_len, layer, q_gain, k_gain, cos, sin):
    t = proj.shape[0]
    tq = min(ATT_Q_TILE, seq_len)
    nq = seq_len // tq
    qw = ATT_GROUP * HEAD_DIM
    kv = lambda off: pl.BlockSpec((seq_len, HEAD_DIM), lambda s, h, i, off=off: (s, off // HEAD_DIM + h))
    gain = pl.BlockSpec((None, 1, HEAD_DIM), lambda s, h, i: (layer, 0, 0))
    table_q = pl.BlockSpec((tq, HEAD_DIM), lambda s, h, i: (i, 0))
    table_k = pl.BlockSpec((seq_len, HEAD_DIM), lambda s, h, i: (0, 0))
    return pl.pallas_call(
        _attn_kernel,
        grid=(t // seq_len, ATT_KV_HEADS, nq),
        in_specs=[pl.BlockSpec((tq, qw), lambda s, h, i: (s * nq + i, OFF_ATT_Q // qw + h)),
                  kv(OFF_ATT_K), kv(OFF_ATT_V), table_q, table_q, table_k, table_k, gain, gain],
        out_specs=pl.BlockSpec((tq, qw), lambda s, h, i: (s * nq + i, h)),
        out_shape=jax.ShapeDtypeStruct((t, ATT_WIDTH), BF16),
        scratch_shapes=[pltpu.VMEM((seq_len, HEAD_DIM), BF16), pltpu.VMEM((seq_len, HEAD_DIM), BF16)],
        compiler_params=_params(("parallel", "parallel", "arbitrary"), 40),
        name="attention",
    )(proj, proj, proj, cos, sin, cos, sin, q_gain, k_gain)


def _merge_kernel(h_ref, ys_ref, yh_ref, ya_ref, g_ref, wb_ref, wo_ref, o_ref):
    acc = h_ref[...]
    ys, yh, ya = ys_ref[...], yh_ref[...], ya_ref[...]
    hg0, att0 = SSM_WIDTH, SSM_WIDTH + HG_WIDTH
    for c in range(D_MODEL // MERGE_TILE):
        cols = slice(c * MERGE_TILE, (c + 1) * MERGE_TILE)
        gate = lambda b: g_ref[:, b * D_MODEL + c * MERGE_TILE:b * D_MODEL + (c + 1) * MERGE_TILE]
        merged = (gate(0) * jnp.dot(ys, wb_ref[0:hg0, cols], preferred_element_type=F32)
                  + gate(1) * jnp.dot(yh, wb_ref[hg0:att0, cols], preferred_element_type=F32)
                  + gate(2) * jnp.dot(ya, wb_ref[att0:, cols], preferred_element_type=F32))
        acc = acc + jnp.dot(merged.astype(BF16), wo_ref[cols, :], preferred_element_type=F32)
    o_ref[...] = acc


def _merge(h, gates, y_ssm, y_hg, y_att, layer, w_branch, w_out):
    t, d = h.shape
    tm = min(TOKEN_TILE, t)
    rows = lambda width: pl.BlockSpec((tm, width), lambda i: (i, 0))
    weight = _resident((None, d, d), lambda i: (layer, 0, 0))
    return pl.pallas_call(
        _merge_kernel,
        grid=(t // tm,),
        in_specs=[rows(d), rows(SSM_WIDTH), rows(HG_WIDTH), rows(ATT_WIDTH), rows(GATE_WIDTH),
                  weight, weight],
        out_specs=rows(d),
        out_shape=jax.ShapeDtypeStruct((t, d), F32),
        compiler_params=_params(("parallel",), 60),
        name="merge",
    )(h, y_ssm, y_hg, y_att, gates, w_branch, w_out)


def _trunk(x, p):
    n_seq, seq_len, d = x.shape
    cos, sin = _rope_tables(seq_len)
    h = x.reshape(n_seq * seq_len, d)
    for layer in range(DEPTH):
        h, u = _ffn(h, layer, p["ffn1_norm"], p["ffn1_w_gate_up"], p["ffn1_w_down"], p["mix_norm"])
        proj = _in_proj(u, layer, p["w_in"], 0, MAIN_WIDTH, False)
        gates = _in_proj(u, layer, p["w_in"], MAIN_WIDTH, GATE_WIDTH, True)
        y_ssm = _s5(proj, seq_len, layer, p["s5_a"], p["s5_bbar"], p["s5_c"], p["ssm_d"], p["ssm_w_glu"])
        y_hg = _hgrn(proj, seq_len, layer, p["hg_lb"], p["hg_out_norm"])
        y_att = _attention(proj, seq_len, layer, p["att_q_norm"], p["att_k_norm"], cos, sin)
        h = _merge(h, gates, y_ssm, y_hg, y_att, layer, p["w_branch"], p["w_out"])
        h = _ffn(h, layer, p["ffn2_norm"], p["ffn2_w_gate_up"], p["ffn2_w_down"])
    return h.reshape(n_seq, seq_len, d)


def kernel(x_prompt, x_sample, ffn1_norm, ffn1_w_gate_up, ffn1_w_down, mix_norm, w_in, ssm_lambda_re, ssm_lambda_im, ssm_log_dt, ssm_b_re, ssm_b_im, ssm_c_re, ssm_c_im, ssm_d, ssm_w_glu, hg_lb_logits, hg_out_norm, att_q_norm, att_k_norm, w_branch, w_out, ffn2_norm, ffn2_w_gate_up, ffn2_w_down):
    s5_a, s5_bbar, s5_c = _s5_prepare(ssm_lambda_re, ssm_lambda_im, ssm_log_dt,
                                      ssm_b_re, ssm_b_im, ssm_c_re, ssm_c_im)
    row = lambda a: a.astype(F32).reshape(DEPTH, 1, a.shape[-1])
    bf16 = lambda a: a.astype(BF16)
    p = dict(
        ffn1_norm=row(ffn1_norm), ffn1_w_gate_up=bf16(ffn1_w_gate_up), ffn1_w_down=bf16(ffn1_w_down),
        mix_norm=row(mix_norm), w_in=bf16(w_in),
        s5_a=s5_a, s5_bbar=s5_bbar, s5_c=s5_c, ssm_d=row(ssm_d), ssm_w_glu=bf16(ssm_w_glu),
        hg_lb=_hg_lower_bound(hg_lb_logits), hg_out_norm=row(hg_out_norm),
        att_q_norm=row(att_q_norm), att_k_norm=row(att_k_norm),
        w_branch=bf16(w_branch), w_out=bf16(w_out),
        ffn2_norm=row(ffn2_norm), ffn2_w_gate_up=bf16(ffn2_w_gate_up), ffn2_w_down=bf16(ffn2_w_down),
    )
    return _trunk(x_prompt, p), _trunk(x_sample, p)
```

```python
import functools
import math

import jax
import jax.numpy as jnp
from jax import lax
from jax.experimental import pallas as pl
from jax.experimental.pallas import tpu as pltpu

F32 = jnp.float32
BF16 = jnp.bfloat16

D_MODEL = 2048
DEPTH = 4
GRID_W = 64
EPS = 1e-6
LB_FLOOR = 1e-30
SSM_WIDTH = 512
SSM_GROUP = 16
SSM_GROUPS = 32
SSM_STATE = 64
SSM_STATES = SSM_GROUPS * SSM_STATE
HG_HEAD_DIM = 128
HG_WIDTH = 512
HG_HEADS = 4
HEAD_DIM = 128
ATT_Q_HEADS = 8
ATT_KV_HEADS = 2
ATT_GROUP = ATT_Q_HEADS // ATT_KV_HEADS
ATT_WIDTH = 1024
KV_WIDTH = 256
ROPE_BASE = 10000.0
ROPE_AXIS_DIM = HEAD_DIM // 2
N_BRANCH = 3
D_FF = 5632
GATE_WIDTH = N_BRANCH * D_MODEL
MAIN_WIDTH = SSM_WIDTH + 5 * HG_WIDTH + ATT_WIDTH + 2 * KV_WIDTH

OFF_SSM = 0
OFF_HG_Q = 512
OFF_HG_FF = 1024
OFF_HG_FB = 1536
OFF_HG_I = 2048
OFF_HG_G = 2560
OFF_ATT_Q = 3072
OFF_ATT_K = 4096
OFF_ATT_V = 4352

LANES = 128
SLABS = SSM_STATES // LANES
S5_HALVES = 2
HALF_WIDTH = SSM_WIDTH // S5_HALVES
HALF_STATES = SSM_STATES // S5_HALVES
HALF_SLABS = SLABS // S5_HALVES

TOKEN_TILE = 512
FF_TILE = 512
PROJ_ROWS = 1024
PROJ_COLS = 1536
MERGE_TILE = 512
ATT_Q_TILE = 256
S5_CHUNK = 256
S5_PITCH = S5_CHUNK + 4
HG_CHUNK = 128
HG_SUB = 16
HG_HEADS_PER_STEP = 2

MIB = 1024 * 1024


def _params(semantics, vmem_mib):
    return pltpu.CompilerParams(dimension_semantics=semantics, vmem_limit_bytes=vmem_mib * MIB)


def _resident(shape, index_map):
    return pl.BlockSpec(shape, index_map, pipeline_mode=pl.Buffered(1))


def _rms_scale(x):
    return x * lax.rsqrt(jnp.mean(x * x, axis=-1, keepdims=True) + EPS)


def _sigmoid(x):
    return 0.5 * jnp.tanh(0.5 * x) + 0.5


def _nt_dot(a, b):
    return lax.dot_general(a, b, (((1,), (1,)), ((), ())), preferred_element_type=F32)


def _ffn_kernel(x_ref, g_ref, wg_ref, wu_ref, wd_ref, *rest, emit_normed):
    if emit_normed:
        g2_ref, o_ref, u_ref, xn_ref, acc_ref = rest
    else:
        o_ref, xn_ref, acc_ref = rest
    f = pl.program_id(1)

    @pl.when(f == 0)
    def _():
        xn_ref[...] = (_rms_scale(x_ref[...]) * g_ref[...]).astype(BF16)
        acc_ref[...] = jnp.zeros_like(acc_ref)

    xn = xn_ref[...]
    half = FF_TILE // 2
    acts = []
    for s in range(2):
        cols = slice(s * half, (s + 1) * half)
        gate = jnp.dot(xn, wg_ref[:, cols], preferred_element_type=F32)
        up = jnp.dot(xn, wu_ref[:, cols], preferred_element_type=F32)
        acts.append((gate * _sigmoid(gate) * up).astype(BF16))
    acc_ref[...] += (jnp.dot(acts[0], wd_ref[0:half, :], preferred_element_type=F32)
                     + jnp.dot(acts[1], wd_ref[half:, :], preferred_element_type=F32))

    @pl.when(f == pl.num_programs(1) - 1)
    def _():
        out = x_ref[...] + 0.5 * acc_ref[...]
        o_ref[...] = out
        if emit_normed:
            u_ref[...] = (_rms_scale(out) * g2_ref[...]).astype(BF16)


def _ffn(x, layer, gain, w_gate_up, w_down, next_gain=None):
    t, d = x.shape
    tm = min(TOKEN_TILE, t)
    nf = D_FF // FF_TILE
    emit = next_gain is not None
    rows = pl.BlockSpec((tm, d), lambda i, f: (i, 0))
    gain_spec = pl.BlockSpec((None, 1, d), lambda i, f: (layer, 0, 0))
    in_specs = [
        rows,
        gain_spec,
        pl.BlockSpec((None, d, FF_TILE), lambda i, f: (layer, 0, f)),
        pl.BlockSpec((None, d, FF_TILE), lambda i, f: (layer, 0, f + nf)),
        pl.BlockSpec((None, FF_TILE, d), lambda i, f: (layer, f, 0)),
    ]
    args = [x, gain, w_gate_up, w_gate_up, w_down]
    out_specs = [rows]
    out_shape = [jax.ShapeDtypeStruct((t, d), F32)]
    if emit:
        in_specs.append(gain_spec)
        args.append(next_gain)
        out_specs.append(rows)
        out_shape.append(jax.ShapeDtypeStruct((t, d), BF16))
    outs = pl.pallas_call(
        functools.partial(_ffn_kernel, emit_normed=emit),
        grid=(t // tm, nf),
        in_specs=in_specs,
        out_specs=out_specs,
        out_shape=out_shape,
        scratch_shapes=[pltpu.VMEM((tm, d), BF16), pltpu.VMEM((tm, d), F32)],
        compiler_params=_params(("parallel", "arbitrary"), 52),
        name="ffn_norm" if emit else "ffn",
    )(*args)
    return outs if emit else outs[0]


def _proj_kernel(a_ref, w_ref, o_ref, *, squash):
    acc = jnp.dot(a_ref[...], w_ref[...], preferred_element_type=F32)
    o_ref[...] = (_sigmoid(acc) if squash else acc).astype(o_ref.dtype)


def _in_proj(u, layer, w_in, first_col, width, squash):
    t, d = u.shape
    tm = min(PROJ_ROWS, t)
    col0 = first_col // PROJ_COLS
    return pl.pallas_call(
        functools.partial(_proj_kernel, squash=squash),
        grid=(t // tm, width // PROJ_COLS),
        in_specs=[pl.BlockSpec((tm, d), lambda i, j: (i, 0)),
                  pl.BlockSpec((None, d, PROJ_COLS), lambda i, j: (layer, 0, col0 + j))],
        out_specs=pl.BlockSpec((tm, PROJ_COLS), lambda i, j: (i, j)),
        out_shape=jax.ShapeDtypeStruct((t, width), BF16 if squash else F32),
        compiler_params=_params(("parallel", "arbitrary"), 40),
        name="gate_proj" if squash else "in_proj",
    )(u, w_in)


def _lower_bound_kernel(logit_ref, lb_ref):
    x = logit_ref[...]
    e = jnp.exp(x - jnp.max(x, axis=0, keepdims=True))
    p = e / jnp.sum(e, axis=0, keepdims=True)
    run = jnp.zeros_like(p[0:1])
    for layer in range(DEPTH):
        run = run + p[layer:layer + 1]
        lb_ref[layer:layer + 1, :] = jnp.clip(run - p[0:1], 0.0, 1.0 - 1e-6)


def _hg_lower_bound(logits):
    flat = logits.astype(F32).reshape(DEPTH, 2 * HG_WIDTH)
    lb = pl.pallas_call(
        _lower_bound_kernel,
        out_shape=jax.ShapeDtypeStruct(flat.shape, F32),
        name="hg_lower_bound",
    )(flat)
    return lb.reshape(DEPTH, 2, HG_WIDTH)


def _s5_prep_kernel(lre_ref, lim_ref, ldt_ref, bre_ref, bim_ref, a_ref, bbar_ref):
    lre = lre_ref[0]
    lim = lim_ref[0]
    dt = jnp.exp(ldt_ref[0])
    mag = jnp.exp(lre * dt)
    a_re = mag * jnp.cos(lim * dt)
    a_im = mag * jnp.sin(lim * dt)
    den = lre * lre + lim * lim
    num_re = a_re - 1.0
    coef_re = (num_re * lre + a_im * lim) / den
    coef_im = (a_im * lre - num_re * lim) / den
    a_ref[0, 0:1, :] = a_re
    a_ref[0, 1:2, :] = a_im
    bre = bre_ref[0]
    bim = bim_ref[0]
    bbar_ref[0, :, 0:HALF_STATES] = (coef_re * bre - coef_im * bim).astype(BF16)
    bbar_ref[0, :, HALF_STATES:] = (coef_re * bim + coef_im * bre).astype(BF16)


def _block_diag_in(b):
    n = b.shape[0]
    per = SSM_GROUPS // S5_HALVES
    b = b.reshape(n * S5_HALVES, per, SSM_STATE, SSM_GROUP)
    full = jnp.einsum('xgnp,gh->xgphn', b, jnp.eye(per, dtype=b.dtype))
    return full.reshape(n * S5_HALVES, HALF_WIDTH, HALF_STATES)


def _block_diag_out(c):
    n = c.shape[0]
    per = SSM_GROUPS // S5_HALVES
    c = c.reshape(n * S5_HALVES, per, SSM_GROUP, SSM_STATE)
    full = jnp.einsum('xgpn,gh->xgnhp', c, jnp.eye(per, dtype=c.dtype))
    return full.reshape(n * S5_HALVES, HALF_STATES, HALF_WIDTH)


def _s5_prepare(lam_re, lam_im, log_dt, b_re, b_im, c_re, c_im):
    n = DEPTH * 2
    m = n * S5_HALVES
    flat = lambda a: a.astype(F32).reshape(m, 1, HALF_STATES)
    ldt = jnp.repeat(log_dt.astype(F32).reshape(n, SSM_GROUPS), SSM_STATE, axis=-1)
    bre = _block_diag_in(b_re.astype(F32).reshape(n, SSM_GROUPS, SSM_STATE, SSM_GROUP))
    bim = _block_diag_in(b_im.astype(F32).reshape(n, SSM_GROUPS, SSM_STATE, SSM_GROUP))
    vec = pl.BlockSpec((1, 1, HALF_STATES), lambda i: (i, 0, 0))
    mat = pl.BlockSpec((1, HALF_WIDTH, HALF_STATES), lambda i: (i, 0, 0))
    a, bbar = pl.pallas_call(
        _s5_prep_kernel,
        grid=(m,),
        in_specs=[vec, vec, vec, mat, mat],
        out_specs=[pl.BlockSpec((1, 2, HALF_STATES), lambda i: (i, 0, 0)),
                   pl.BlockSpec((1, HALF_WIDTH, 2 * HALF_STATES), lambda i: (i, 0, 0))],
        out_shape=[jax.ShapeDtypeStruct((m, 2, HALF_STATES), F32),
                   jax.ShapeDtypeStruct((m, HALF_WIDTH, 2 * HALF_STATES), BF16)],
        compiler_params=_params(("parallel",), 32),
        name="s5_prep",
    )(flat(lam_re), flat(lam_im), flat(ldt), bre, bim)
    a = a.reshape(DEPTH, 2, S5_HALVES, 2, HALF_STATES).transpose(0, 1, 3, 2, 4)
    a = a.reshape(DEPTH, 2, 2, SLABS, LANES)
    bbar = bbar.reshape(DEPTH, 2, S5_HALVES, HALF_WIDTH, 2 * HALF_STATES)
    cre = _block_diag_out(c_re.astype(F32).reshape(n, SSM_GROUPS, SSM_GROUP, SSM_STATE))
    cim = _block_diag_out(c_im.astype(F32).reshape(n, SSM_GROUPS, SSM_GROUP, SSM_STATE))
    cmat = jnp.stack([cre.reshape(DEPTH, 2, S5_HALVES, HALF_STATES, HALF_WIDTH),
                      cim.reshape(DEPTH, 2, S5_HALVES, HALF_STATES, HALF_WIDTH)], axis=2)
    return a, bbar, cmat.astype(BF16)


def _s5_kernel(u_ref, a_ref, bbar_ref, c_ref, d_ref, wglu_ref, o_ref, y_ref, sf_ref, sb_ref, xf_ref, xb_ref):
    seq_len = u_ref.shape[0]
    q = min(S5_CHUNK, seq_len)
    n_chunks = seq_len // q
    im0 = SLABS * S5_PITCH
    y_ref[...] = d_ref[...] * u_ref[...]
    a_f_re, a_f_im = a_ref[0, 0], a_ref[0, 1]
    a_b_re, a_b_im = a_ref[1, 0], a_ref[1, 1]

    def slab_rows(j):
        return slice(j * S5_PITCH, j * S5_PITCH + q)

    def project_in(rows, direction, slab_ref):
        for half in range(S5_HALVES):
            u_half = u_ref[rows, half * HALF_WIDTH:(half + 1) * HALF_WIDTH].astype(BF16)
            bu = jnp.dot(u_half, bbar_ref[direction, half], preferred_element_type=F32)
            for part in range(2):
                for j in range(HALF_SLABS):
                    col = part * HALF_STATES + j * LANES
                    slab_ref[slab_rows(part * SLABS + half * HALF_SLABS + j), :] = bu[:, col:col + LANES]

    def project_out(direction, slab_ref):
        halves = []
        for half in range(S5_HALVES):
            def natural(part):
                first = part * SLABS + half * HALF_SLABS
                return jnp.concatenate([slab_ref[slab_rows(first + j), :] for j in range(HALF_SLABS)],
                                       axis=1).astype(BF16)
            halves.append(jnp.dot(natural(0), c_ref[direction, 0, half], preferred_element_type=F32)
                          - jnp.dot(natural(1), c_ref[direction, 1, half], preferred_element_type=F32))
        return jnp.concatenate(halves, axis=1)

    def advance(slab_ref, state_ref, t, a_re, a_im, x_re, x_im):
        re_rows = pl.ds(t, SLABS, stride=S5_PITCH)
        im_rows = pl.ds(im0 + t, SLABS, stride=S5_PITCH)
        n_re = a_re * x_re - a_im * x_im + slab_ref[re_rows, :]
        n_im = a_re * x_im + a_im * x_re + slab_ref[im_rows, :]
        state_ref[re_rows, :] = n_re
        state_ref[im_rows, :] = n_im
        return n_re, n_im

    def chunk(c, carry):
        rows_f = pl.ds(pl.multiple_of(c * q, q), q)
        rows_b = pl.ds(pl.multiple_of((n_chunks - 1 - c) * q, q), q)
        project_in(rows_f, 0, sf_ref)
        project_in(rows_b, 1, sb_ref)

        def step(i, st):
            f_re, f_im = advance(sf_ref, xf_ref, i, a_f_re, a_f_im, st[0], st[1])
            b_re, b_im = advance(sb_ref, xb_ref, q - 1 - i, a_b_re, a_b_im, st[2], st[3])
            return f_re, f_im, b_re, b_im

        carry = lax.fori_loop(0, q, step, carry, unroll=4)
        y_ref[rows_f, :] += project_out(0, xf_ref)
        y_ref[rows_b, :] += project_out(1, xb_ref)
        return carry

    zero = jnp.zeros((SLABS, LANES), F32)
    lax.fori_loop(0, n_chunks, chunk, (zero, zero, zero, zero))
    z = jax.nn.gelu(y_ref[...])
    glu = jnp.dot(z.astype(BF16), wglu_ref[...], preferred_element_type=F32)
    o_ref[...] = (z * _sigmoid(glu)).astype(o_ref.dtype)


def _s5(proj, seq_len, layer, a, bbar, cmat, d_skip, w_glu):
    t = proj.shape[0]

    def whole(shape):
        return _resident((None,) + shape, lambda s: (layer,) + (0,) * len(shape))

    return pl.pallas_call(
        _s5_kernel,
        grid=(t // seq_len,),
        in_specs=[
            pl.BlockSpec((seq_len, SSM_WIDTH), lambda s: (s, OFF_SSM // SSM_WIDTH)),
            whole((2, 2, SLABS, LANES)),
            whole((2, S5_HALVES, HALF_WIDTH, 2 * HALF_STATES)),
            whole((2, 2, S5_HALVES, HALF_STATES, HALF_WIDTH)),
            whole((1, SSM_WIDTH)),
            whole((SSM_WIDTH, SSM_WIDTH)),
        ],
        out_specs=pl.BlockSpec((seq_len, SSM_WIDTH), lambda s: (s, 0)),
        out_shape=jax.ShapeDtypeStruct((t, SSM_WIDTH), BF16),
        scratch_shapes=[pltpu.VMEM((seq_len, SSM_WIDTH), F32)]
        + [pltpu.VMEM((2 * SLABS * S5_PITCH, LANES), F32)] * 4,
        compiler_params=_params(("parallel",), 56),
        name="s5",
    )(proj, a, bbar, cmat, d_skip, w_glu)


def _hg_gates(z, lb):
    e = jnp.exp(-jnp.abs(z))
    r = 1.0 / (1.0 + e)
    er = e * r
    pos = z >= 0.0
    log_f = jnp.log(jnp.maximum(lb, LB_FLOOR) + (1.0 - lb) * jnp.where(pos, r, er))
    return log_f, (1.0 - lb) * jnp.where(pos, er, r)


def _pair_diag(x):
    lane = lax.broadcasted_iota(jnp.int32, x.shape, 1)
    zero = jnp.zeros_like(x)
    return jnp.concatenate([jnp.where(lane < HG_HEAD_DIM, x, zero),
                            jnp.where(lane >= HG_HEAD_DIM, x, zero)], axis=0)


def _hg_reach(c, reverse):
    row = lax.broadcasted_iota(jnp.int32, (c, c), 0)
    col = lax.broadcasted_iota(jnp.int32, (c, c), 1)
    return (col >= row) if reverse else (col <= row)


def _hg_prepare(q_raw, z, vc, lb, reverse):
    c = q_raw.shape[0]
    qc = q_raw * _sigmoid(q_raw)
    lfc, kc = _hg_gates(z, lb)
    ones = jnp.where(_hg_reach(c, reverse), 1.0, 0.0).astype(BF16)
    part0 = lfc.astype(BF16)
    rest = lfc - part0.astype(F32)
    part1 = rest.astype(BF16)
    part2 = (rest - part1.astype(F32)).astype(BF16)
    sums = jnp.dot(ones, jnp.concatenate([part0, part1, part2], axis=1), preferred_element_type=F32)
    w = lfc.shape[1]
    cum = sums[:, 0:w] + sums[:, w:2 * w] + sums[:, 2 * w:3 * w]
    total = cum[0:1] if reverse else cum[c - 1:c]
    q_state = (qc * jnp.exp(cum)).astype(BF16)
    k_state = (kc * jnp.exp(total - cum)).astype(BF16)
    v_t = jnp.concatenate([vc[:, 0:HG_HEAD_DIM].T, vc[:, HG_HEAD_DIM:].T], axis=1).astype(BF16)
    return qc, kc, cum, cum - lfc, q_state, k_state, v_t, vc.astype(BF16), jnp.exp(total)


def _hg_state_step(prep, state_t):
    _, _, _, _, q_state, k_state, v_t, _, decay = prep
    out = _nt_dot(q_state, _pair_diag(state_t.astype(BF16)))
    return out, state_t * decay + jnp.dot(v_t, _pair_diag(k_state), preferred_element_type=F32)


def _hg_block_operands(d, step):
    c = d["qc"].shape[0]
    n_blk = c // HG_SUB
    reverse = d["reverse"]
    blk = n_blk - 1 - step if reverse else step
    lo, hi = blk * HG_SUB, (blk + 1) * HG_SUB
    cum = d["cum"]
    ref_row = d["cum_before"][hi - 1:hi] if reverse else d["cum_before"][lo:lo + 1]
    fresh = d["kc"][lo:hi] * jnp.exp(ref_row - cum[lo:hi])
    if d["keys"] is None:
        keys = fresh
    else:
        moved = d["keys"] * jnp.exp(ref_row - d["prev_ref"])
        keys = jnp.concatenate([fresh, moved] if reverse else [moved, fresh], axis=0)
    d["keys"], d["prev_ref"] = keys, ref_row
    padded = keys
    if keys.shape[0] < c:
        blank = jnp.zeros((c - keys.shape[0], keys.shape[1]), F32)
        padded = jnp.concatenate([blank, keys] if reverse else [keys, blank], axis=0)
    q_blk = (d["qc"][lo:hi] * jnp.exp(cum[lo:hi] - ref_row)).astype(BF16)
    return blk, _pair_diag(q_blk), padded.astype(BF16)


def _hg_scores(prep_f, prep_b):
    c = prep_f[0].shape[0]
    n_blk = c // HG_SUB
    dirs = [dict(qc=p[0], kc=p[1], cum=p[2], cum_before=p[3], reverse=rev, keys=None, prev_ref=None,
                 rows=[[None] * n_blk for _ in range(HG_HEADS_PER_STEP)])
            for p, rev in ((prep_f, False), (prep_b, True))]
    for step in range(n_blk):
        blk_f, q_f, k_f = _hg_block_operands(dirs[0], step)
        blk_b, q_b, k_b = _hg_block_operands(dirs[1], step)
        res = _nt_dot(jnp.concatenate([q_f, q_b], axis=0), jnp.concatenate([k_f, k_b], axis=0))
        for head in range(HG_HEADS_PER_STEP):
            dirs[0]["rows"][head][blk_f] = res[head * HG_SUB:(head + 1) * HG_SUB, 0:c]
            dirs[1]["rows"][head][blk_b] = res[(2 + head) * HG_SUB:(3 + head) * HG_SUB, c:2 * c]
    return [jnp.concatenate(
        [jnp.where(_hg_reach(c, d["reverse"]), jnp.concatenate(rows, axis=0), 0.0).astype(BF16)
         for rows in d["rows"]], axis=1) for d in dirs]


def _hgrn_kernel(q_ref, ff_ref, fb_ref, i_ref, g_ref, lb_ref, gain_ref, o_ref, acc_ref,
                 wide_a, wide_b, narrow_a, narrow_b, decay_a, decay_b):
    seq_len = q_ref.shape[0]
    c = HG_CHUNK
    n_chunks = seq_len // c
    width = HG_HEADS_PER_STEP * HG_HEAD_DIM
    acc_ref[...] = jnp.zeros_like(acc_ref)
    slots = ((wide_a, narrow_a, decay_a), (wide_b, narrow_b, decay_b))

    def rows_of(ci):
        return pl.ds(pl.multiple_of(ci * c, c), c)

    def prepare(ci, slot):
        wide, narrow, decay = slots[slot]
        rows_f, rows_b = rows_of(ci), rows_of(n_chunks - 1 - ci)
        preps = (_hg_prepare(q_ref[rows_f, :], ff_ref[rows_f, :], i_ref[rows_f, :], lb_ref[0:1, :], False),
                 _hg_prepare(q_ref[rows_b, :], fb_ref[rows_b, :], i_ref[rows_b, :], lb_ref[1:2, :], True))
        for d, prep in enumerate(preps):
            for k in range(4):
                wide[d, k] = prep[k]
                narrow[d, k] = prep[4 + k]
            decay[d] = jnp.broadcast_to(prep[8], decay.shape[1:])

    def prepared(slot, d):
        wide, narrow, decay = slots[slot]
        return (tuple(wide[d, k] for k in range(4)) + tuple(narrow[d, k] for k in range(4))
                + (decay[d, 0:1],))

    def finish(ci, slot):
        wide, narrow, _ = slots[slot]
        for d, rows in enumerate((rows_of(ci), rows_of(n_chunks - 1 - ci))):
            acc_ref[rows, :] += wide[d, 4] + jnp.dot(narrow[d, 4], _pair_diag(narrow[d, 3]),
                                                     preferred_element_type=F32)

    def middle(slot, state_f, state_b):
        wide, narrow, _ = slots[slot]
        prep_f, prep_b = prepared(slot, 0), prepared(slot, 1)
        out_f, state_f = _hg_state_step(prep_f, state_f)
        out_b, state_b = _hg_state_step(prep_b, state_b)
        scores_f, scores_b = _hg_scores(prep_f, prep_b)
        wide[0, 4], wide[1, 4] = out_f, out_b
        narrow[0, 4], narrow[1, 4] = scores_f, scores_b
        return state_f, state_b

    def pair(j, states):
        for slot in range(2):
            ci = 2 * j + slot
            states = middle(slot, *states)
            finish(ci, slot)
            prepare(jnp.minimum(ci + 2, n_chunks - 1), slot)
        return states

    prepare(0, 0)
    prepare(1, 1)
    state0 = jnp.zeros((HG_HEAD_DIM, width), F32)
    lax.fori_loop(0, n_chunks // 2, pair, (state0, state0))
    for head in range(HG_HEADS_PER_STEP):
        lanes = slice(head * HG_HEAD_DIM, (head + 1) * HG_HEAD_DIM)
        g_raw = g_ref[:, lanes]
        normed = _rms_scale(acc_ref[:, lanes]) * gain_ref[...]
        o_ref[:, lanes] = (normed * (g_raw * _sigmoid(g_raw))).astype(o_ref.dtype)


def _hgrn(proj, seq_len, layer, lower_bound, out_gain):
    t = proj.shape[0]
    width = HG_HEADS_PER_STEP * HG_HEAD_DIM
    col = lambda off: pl.BlockSpec((seq_len, width), lambda s, h, off=off: (s, off // width + h))
    return pl.pallas_call(
        _hgrn_kernel,
        grid=(t // seq_len, HG_HEADS // HG_HEADS_PER_STEP),
        in_specs=[col(OFF_HG_Q), col(OFF_HG_FF), col(OFF_HG_FB), col(OFF_HG_I), col(OFF_HG_G),
                  pl.BlockSpec((None, 2, width), lambda s, h: (layer, 0, h)),
                  pl.BlockSpec((None, 1, HG_HEAD_DIM), lambda s, h: (layer, 0, 0))],
        out_specs=pl.BlockSpec((seq_len, width), lambda s, h: (s, h)),
        out_shape=jax.ShapeDtypeStruct((t, HG_WIDTH), BF16),
        scratch_shapes=[pltpu.VMEM((seq_len, width), F32)]
        + [pltpu.VMEM((2, 5, HG_CHUNK, width), F32)] * 2
        + [pltpu.VMEM((2, 5, HG_CHUNK, width), BF16)] * 2
        + [pltpu.VMEM((2, 8, width), F32)] * 2,
        compiler_params=_params(("parallel", "parallel"), 40),
        name="hgrn2",
    )(proj, proj, proj, proj, proj, lower_bound, out_gain)


def _rope_tables(seq_len):
    pos = jnp.arange(seq_len, dtype=jnp.int32)
    row = (pos // GRID_W).astype(F32)
    colp = (pos % GRID_W).astype(F32)
    inv_freq = ROPE_BASE ** (-jnp.arange(0, ROPE_AXIS_DIM, 2, dtype=F32) / ROPE_AXIS_DIM)
    ang_r = row[:, None] * inv_freq[None, :]
    ang_c = colp[:, None] * inv_freq[None, :]
    cos = jnp.concatenate([jnp.cos(ang_r)] * 2 + [jnp.cos(ang_c)] * 2, axis=-1)
    sin = jnp.concatenate([-jnp.sin(ang_r), jnp.sin(ang_r), -jnp.sin(ang_c), jnp.sin(ang_c)], axis=-1)
    return cos, sin


def _rope(x, cos, sin):
    half = ROPE_AXIS_DIM // 2
    lane = lax.broadcasted_iota(jnp.int32, x.shape, 1)
    partner = jnp.where(lane % ROPE_AXIS_DIM < half,
                        pltpu.roll(x, HEAD_DIM - half, axis=1), pltpu.roll(x, half, axis=1))
    return x * cos + partner * sin


def _attn_kernel(q_ref, k_ref, v_ref, cosq_ref, sinq_ref, cosk_ref, sink_ref, qg_ref, kg_ref,
                 o_ref, ks_ref, vs_ref):
    @pl.when(pl.program_id(2) == 0)
    def _():
        kn = _rms_scale(k_ref[...]) * kg_ref[...]
        ks_ref[...] = _rope(kn, cosk_ref[...], sink_ref[...]).astype(BF16)
        vs_ref[...] = v_ref[...].astype(BF16)

    scale = HEAD_DIM ** -0.5 * math.log2(math.e)
    keys = ks_ref[...]
    vals = vs_ref[...]

    def logits(g):
        qn = _rms_scale(q_ref[:, g * HEAD_DIM:(g + 1) * HEAD_DIM]) * qg_ref[...]
        return _nt_dot((_rope(qn, cosq_ref[...], sinq_ref[...]) * scale).astype(BF16), keys)

    s_next = logits(0)
    for g in range(ATT_GROUP):
        s = s_next
        if g + 1 < ATT_GROUP:
            s_next = logits(g + 1)
        p = jnp.exp2(s - jnp.max(s, axis=-1, keepdims=True))
        denom = jnp.sum(p, axis=-1, keepdims=True)
        pv = jnp.dot(p.astype(BF16), vals, preferred_element_type=F32)
        o_ref[:, g * HEAD_DIM:(g + 1) * HEAD_DIM] = (pv / denom).astype(o_ref.dtype)


def _attention(proj, seq_len, layer, q_gain, k_gain, cos, sin):
    t = proj.shape[0]
    tq = min(ATT_Q_TILE, seq_len)
    nq = seq_len // tq
    qw = ATT_GROUP * HEAD_DIM
    kv = lambda off: pl.BlockSpec((seq_len, HEAD_DIM), lambda s, h, i, off=off: (s, off // HEAD_DIM + h))
    gain = pl.BlockSpec((None, 1, HEAD_DIM), lambda s, h, i: (layer, 0, 0))
    table_q = pl.BlockSpec((tq, HEAD_DIM), lambda s, h, i: (i, 0))
    table_k = pl.BlockSpec((seq_len, HEAD_DIM), lambda s, h, i: (0, 0))
    return pl.pallas_call(
        _attn_kernel,
        grid=(t // seq_len, ATT_KV_HEADS, nq),
        in_specs=[pl.BlockSpec((tq, qw), lambda s, h, i: (s * nq + i, OFF_ATT_Q // qw + h)),
                  kv(OFF_ATT_K), kv(OFF_ATT_V), table_q, table_q, table_k, table_k, gain, gain],
        out_specs=pl.BlockSpec((tq, qw), lambda s, h, i: (s * nq + i, h)),
        out_shape=jax.ShapeDtypeStruct((t, ATT_WIDTH), BF16),
        scratch_shapes=[pltpu.VMEM((seq_len, HEAD_DIM), BF16), pltpu.VMEM((seq_len, HEAD_DIM), BF16)],
        compiler_params=_params(("parallel", "parallel", "arbitrary"), 40),
        name="attention",
    )(proj, proj, proj, cos, sin, cos, sin, q_gain, k_gain)


def _merge_kernel(h_ref, ys_ref, yh_ref, ya_ref, g_ref, wb_ref, wo_ref, o_ref):
    acc = h_ref[...]
    ys, yh, ya = ys_ref[...], yh_ref[...], ya_ref[...]
    hg0, att0 = SSM_WIDTH, SSM_WIDTH + HG_WIDTH
    for c in range(D_MODEL // MERGE_TILE):
        cols = slice(c * MERGE_TILE, (c + 1) * MERGE_TILE)
        gate = lambda b: g_ref[:, b * D_MODEL + c * MERGE_TILE:b * D_MODEL + (c + 1) * MERGE_TILE]
        merged = (gate(0) * jnp.dot(ys, wb_ref[0:hg0, cols], preferred_element_type=F32)
                  + gate(1) * jnp.dot(yh, wb_ref[hg0:att0, cols], preferred_element_type=F32)
                  + gate(2) * jnp.dot(ya, wb_ref[att0:, cols], preferred_element_type=F32))
        acc = acc + jnp.dot(merged.astype(BF16), wo_ref[cols, :], preferred_element_type=F32)
    o_ref[...] = acc


def _merge(h, gates, y_ssm, y_hg, y_att, layer, w_branch, w_out):
    t, d = h.shape
    tm = min(TOKEN_TILE, t)
    rows = lambda width: pl.BlockSpec((tm, width), lambda i: (i, 0))
    weight = _resident((None, d, d), lambda i: (layer, 0, 0))
    return pl.pallas_call(
        _merge_kernel,
        grid=(t // tm,),
        in_specs=[rows(d), rows(SSM_WIDTH), rows(HG_WIDTH), rows(ATT_WIDTH), rows(GATE_WIDTH),
                  weight, weight],
        out_specs=rows(d),
        out_shape=jax.ShapeDtypeStruct((t, d), F32),
        compiler_params=_params(("parallel",), 60),
        name="merge",
    )(h, y_ssm, y_hg, y_att, gates, w_branch, w_out)


def _trunk(x, p):
    n_seq, seq_len, d = x.shape
    cos, sin = _rope_tables(seq_len)
    h = x.reshape(n_seq * seq_len, d)
    for layer in range(DEPTH):
        h, u = _ffn(h, layer, p["ffn1_norm"], p["ffn1_w_gate_up"], p["ffn1_w_down"], p["mix_norm"])
        proj = _in_proj(u, layer, p["w_in"], 0, MAIN_WIDTH, False)
        gates = _in_proj(u, layer, p["w_in"], MAIN_WIDTH, GATE_WIDTH, True)
        y_ssm = _s5(proj, seq_len, layer, p["s5_a"], p["s5_bbar"], p["s5_c"], p["ssm_d"], p["ssm_w_glu"])
        y_hg = _hgrn(proj, seq_len, layer, p["hg_lb"], p["hg_out_norm"])
        y_att = _attention(proj, seq_len, layer, p["att_q_norm"], p["att_k_norm"], cos, sin)
        h = _merge(h, gates, y_ssm, y_hg, y_att, layer, p["w_branch"], p["w_out"])
        h = _ffn(h, layer, p["ffn2_norm"], p["ffn2_w_gate_up"], p["ffn2_w_down"])
    return h.reshape(n_seq, seq_len, d)


def kernel(x_prompt, x_sample, ffn1_norm, ffn1_w_gate_up, ffn1_w_down, mix_norm, w_in, ssm_lambda_re, ssm_lambda_im, ssm_log_dt, ssm_b_re, ssm_b_im, ssm_c_re, ssm_c_im, ssm_d, ssm_w_glu, hg_lb_logits, hg_out_norm, att_q_norm, att_k_norm, w_branch, w_out, ffn2_norm, ffn2_w_gate_up, ffn2_w_down):
    s5_a, s5_bbar, s5_c = _s5_prepare(ssm_lambda_re, ssm_lambda_im, ssm_log_dt,
                                      ssm_b_re, ssm_b_im, ssm_c_re, ssm_c_im)
    row = lambda a: a.astype(F32).reshape(DEPTH, 1, a.shape[-1])
    bf16 = lambda a: a.astype(BF16)
    p = dict(
        ffn1_norm=row(ffn1_norm), ffn1_w_gate_up=bf16(ffn1_w_gate_up), ffn1_w_down=bf16(ffn1_w_down),
        mix_norm=row(mix_norm), w_in=bf16(w_in),
        s5_a=s5_a, s5_bbar=s5_bbar, s5_c=s5_c, ssm_d=row(ssm_d), ssm_w_glu=bf16(ssm_w_glu),
        hg_lb=_hg_lower_bound(hg_lb_logits), hg_out_norm=row(hg_out_norm),
        att_q_norm=row(att_q_norm), att_k_norm=row(att_k_norm),
        w_branch=bf16(w_branch), w_out=bf16(w_out),
        ffn2_norm=row(ffn2_norm), ffn2_w_gate_up=bf16(ffn2_w_gate_up), ffn2_w_down=bf16(ffn2_w_down),
    )
    return _trunk(x_prompt, p), _trunk(x_sample, p)
```
